```python
import math
import jax, jax.numpy as jnp
from jax import lax
import numpy as np

D_MODEL = 1024
BATCH = 1
SEQ = 16384
DEPTH = 4

CHUNK = 64
MIX_A = D_MODEL // 2
MIX_B = D_MODEL // 2
MIX_C = D_MODEL // 2
MIX_D = D_MODEL // 2
S5_GROUP = 16
S5_GROUPS = MIX_A // S5_GROUP
S5_STATE = 64
CONV_W = 3
ATT_HEADS = 8
HEAD_DIM = MIX_C // ATT_HEADS
LEFT_CHUNKS = 8
BAND = (LEFT_CHUNKS + 1) * CHUNK
MAX_REL = 128
POOL_WINDOWS = (2, 4, 8, 16)
POOL_GROUP = MIX_D // len(POOL_WINDOWS)
D_FF = ((math.ceil(8 * D_MODEL / 3) + 255) // 256) * 256
D_PLE = 256
EV_IN = MIX_A + 3 * MIX_B
OD_IN = 3 * MIX_C + MIX_D
N_EVEN = (DEPTH + 1) // 2
N_ODD = DEPTH // 2
ALPHA = (2 * DEPTH) ** 0.25
BETA = (8 * DEPTH) ** -0.25
LN_EPS = 1e-5
NEG_INF = -1e30

kernel_name = "hybrid_s5_conv_chunkattn_pool_deepnorm"


def layer_norm(x, g, b):
    xf = x.astype(jnp.float32)
    mu = jnp.mean(xf, axis=-1, keepdims=True)
    var = jnp.mean(jnp.square(xf - mu), axis=-1, keepdims=True)
    y = (xf - mu) * lax.rsqrt(var + LN_EPS)
    return (y * g.astype(jnp.float32) + b.astype(jnp.float32)).astype(x.dtype)


def _complex_affine_combine(e1, e2):
    a1r, a1i, b1r, b1i = e1
    a2r, a2i, b2r, b2i = e2
    ar = a2r * a1r - a2i * a1i
    ai = a2r * a1i + a2i * a1r
    br = a2r * b1r - a2i * b1i + b2r
    bi = a2r * b1i + a2i * b1r + b2i
    return (ar, ai, br, bi)


def s5_mixer(u, lam_re, lam_im, log_dt, b_re, b_im, c_re, c_im, d_skip, w_glu, b_glu):
    f32 = jnp.float32
    bsz, L, _ = u.shape
    lre = lam_re.astype(f32)
    lim = lam_im.astype(f32)
    dt = jnp.exp(log_dt.astype(f32))[:, None]
    mag = jnp.exp(lre * dt)
    ang = lim * dt
    lb_re = mag * jnp.cos(ang)
    lb_im = mag * jnp.sin(ang)
    den = lre * lre + lim * lim
    nr = lb_re - 1.0
    ni = lb_im
    r_re = (nr * lre + ni * lim) / den
    r_im = (ni * lre - nr * lim) / den
    br = b_re.astype(f32)
    bi = b_im.astype(f32)
    bb_re = r_re[..., None] * br - r_im[..., None] * bi
    bb_im = r_re[..., None] * bi + r_im[..., None] * br
    uf = u.astype(f32).reshape(bsz, L, S5_GROUPS, S5_GROUP)
    bu_re = jnp.einsum('blgh,gph->blgp', uf, bb_re)
    bu_im = jnp.einsum('blgh,gph->blgp', uf, bb_im)
    a_re = jnp.broadcast_to(lb_re, bu_re.shape)
    a_im = jnp.broadcast_to(lb_im, bu_im.shape)
    _, _, xr, xi = lax.associative_scan(_complex_affine_combine, (a_re, a_im, bu_re, bu_im), axis=1)
    y = (jnp.einsum('ghp,blgp->blgh', c_re.astype(f32), xr)
         - jnp.einsum('ghp,blgp->blgh', c_im.astype(f32), xi)
         + d_skip.astype(f32).reshape(S5_GROUPS, S5_GROUP) * uf)
    y = y.reshape(bsz, L, MIX_A)
    g = jax.nn.gelu(y)
    out = g * jax.nn.sigmoid(g @ w_glu.astype(f32) + b_glu.astype(f32))
    return out.astype(u.dtype)


def short_conv_mixer(b_gate, c_gate, x_in, conv_w):
    L = x_in.shape[1]
    z = c_gate * x_in
    zp = jnp.pad(z, ((0, 0), (CONV_W - 1, 0), (0, 0)))
    y = conv_w[0] * zp[:, 0:L]
    for k in range(1, CONV_W):
        y = y + conv_w[k] * zp[:, k:k + L]
    return b_gate * y


def chunk_attention(q, k, v, rel_bias):
    bsz, L, _ = q.shape
    nc = L // CHUNK
    q = q.reshape(bsz, nc, CHUNK, ATT_HEADS, HEAD_DIM) * (HEAD_DIM ** -0.5)
    k = k.reshape(bsz, nc, CHUNK, ATT_HEADS, HEAD_DIM)
    v = v.reshape(bsz, nc, CHUNK, ATT_HEADS, HEAD_DIM)
    pad = ((0, 0), (LEFT_CHUNKS, 0), (0, 0), (0, 0), (0, 0))
    kp = jnp.pad(k, pad)
    vp = jnp.pad(v, pad)
    kb = jnp.concatenate([kp[:, j:j + nc] for j in range(LEFT_CHUNKS + 1)], axis=2)
    vb = jnp.concatenate([vp[:, j:j + nc] for j in range(LEFT_CHUNKS + 1)], axis=2)
    s = jnp.einsum('bcqhd,bckhd->bchqk', q, kb).astype(jnp.float32)
    qi = jnp.arange(CHUNK)[:, None]
    kj = jnp.arange(BAND)[None, :]
    rel = jnp.clip(qi + LEFT_CHUNKS * CHUNK - kj, -MAX_REL, MAX_REL) + MAX_REL
    bias = rel_bias.astype(jnp.float32)[:, rel]
    key_chunk = (jnp.arange(BAND) // CHUNK)[None, :] - LEFT_CHUNKS
    valid = (jnp.arange(nc)[:, None] + key_chunk) >= 0
    s = jnp.where(valid[None, :, None, None, :], s + bias[None, None], NEG_INF)
    pr = jax.nn.softmax(s, axis=-1).astype(vb.dtype)
    o = jnp.einsum('bchqk,bckhd->bcqhd', pr, vb)
    return o.reshape(bsz, L, MIX_C)


def pool_mixer(z, pool_w, pool_scale):
    bsz, L, _ = z.shape
    zf = z.astype(jnp.float32)
    cs = jnp.cumsum(zf, axis=1)
    t = jnp.arange(L)
    outs = []
    for gi, w in enumerate(POOL_WINDOWS):
        lo, hi = gi * POOL_GROUP, (gi + 1) * POOL_GROUP
        csg = cs[..., lo:hi]
        lagged = jnp.pad(csg[:, :L - w], ((0, 0), (w, 0), (0, 0)))
        count = jnp.minimum(t + 1, w).astype(jnp.float32)[None, :, None]
        outs.append((csg - lagged) / count - zf[..., lo:hi])
    pooled = jnp.stack(outs, axis=2)
    mixed = jnp.einsum('blgc,gcd->blgd', pooled, pool_w.astype(jnp.float32)).reshape(bsz, L, MIX_D)
    return (mixed * pool_scale.astype(jnp.float32)).astype(z.dtype)


def swiglu(x, w_up, w_down):
    h = x @ w_up
    g, u = jnp.split(h, 2, axis=-1)
    return (jax.nn.silu(g) * u) @ w_down


def setup_inputs(seed: int = 0) -> dict:
    key = jax.random.key(seed)
    ks = jax.random.split(key, 32)
    f32 = jnp.float32

    def nrm(k, shape, scale):
        return scale * jax.random.normal(k, shape, f32)

    x = nrm(ks[0], (BATCH, SEQ, D_MODEL), 1.0)
    p = nrm(ks[1], (DEPTH, BATCH, SEQ, D_PLE), 1.0)
    ev_w_in = nrm(ks[2], (N_EVEN, D_MODEL, EV_IN), D_MODEL ** -0.5)
    n_idx = jnp.arange(S5_STATE, dtype=f32)
    ev_lambda_re = -0.5 + nrm(ks[3], (N_EVEN, S5_GROUPS, S5_STATE), 0.01)
    ev_lambda_im = math.pi * n_idx + nrm(ks[4], (N_EVEN, S5_GROUPS, S5_STATE), 0.01)
    ev_log_dt = jax.random.uniform(ks[5], (N_EVEN, S5_GROUPS), f32, math.log(1e-3), math.log(1e-1))
    ev_b_re = nrm(ks[6], (N_EVEN, S5_GROUPS, S5_STATE, S5_GROUP), (2 * S5_GROUP) ** -0.5)
    ev_b_im = nrm(ks[7], (N_EVEN, S5_GROUPS, S5_STATE, S5_GROUP), (2 * S5_GROUP) ** -0.5)
    ev_c_re = nrm(ks[8], (N_EVEN, S5_GROUPS, S5_GROUP, S5_STATE), S5_STATE ** -0.5)
    ev_c_im = nrm(ks[9], (N_EVEN, S5_GROUPS, S5_GROUP, S5_STATE), S5_STATE ** -0.5)
    ev_d = nrm(ks[10], (N_EVEN, MIX_A), 1.0)
    ev_w_glu = nrm(ks[11], (N_EVEN, MIX_A, MIX_A), MIX_A ** -0.5)
    ev_b_glu = nrm(ks[12], (N_EVEN, MIX_A), 0.02)
    ev_conv_w = nrm(ks[13], (N_EVEN, CONV_W, MIX_B), CONV_W ** -0.5)
    ev_w_out = nrm(ks[14], (N_EVEN, D_MODEL, D_MODEL), BETA * D_MODEL ** -0.5)
    od_w_in = nrm(ks[15], (N_ODD, D_MODEL, OD_IN), D_MODEL ** -0.5)
    od_rel_bias = nrm(ks[16], (N_ODD, ATT_HEADS, 2 * MAX_REL + 1), 0.1)
    od_pool_w = nrm(ks[17], (N_ODD, len(POOL_WINDOWS), POOL_GROUP, POOL_GROUP), POOL_GROUP ** -0.5)
    od_pool_scale = 1.0 + nrm(ks[18], (N_ODD, MIX_D), 0.1)
    od_w_out = nrm(ks[19], (N_ODD, D_MODEL, D_MODEL), BETA * D_MODEL ** -0.5)
    ln_mix_g = 1.0 + nrm(ks[20], (DEPTH, D_MODEL), 0.01)
    ln_mix_b = nrm(ks[21], (DEPTH, D_MODEL), 0.01)
    ln_ffn_g = 1.0 + nrm(ks[22], (DEPTH, D_MODEL), 0.01)
    ln_ffn_b = nrm(ks[23], (DEPTH, D_MODEL), 0.01)
    ffn_w_up = nrm(ks[24], (DEPTH, D_MODEL, 2 * D_FF), D_MODEL ** -0.5)
    ffn_w_down = nrm(ks[25], (DEPTH, D_FF, D_MODEL), BETA * D_FF ** -0.5)
    ple_w_proj = nrm(ks[26], (DEPTH, D_PLE, D_MODEL), D_PLE ** -0.5)
    ple_w_gate = nrm(ks[27], (DEPTH, D_MODEL, D_MODEL), D_MODEL ** -0.5)
    ple_b_gate = nrm(ks[28], (DEPTH, D_MODEL), 0.01)
    return {
        "x": x, "p": p,
        "ev_w_in": ev_w_in, "ev_lambda_re": ev_lambda_re, "ev_lambda_im": ev_lambda_im,
        "ev_log_dt": ev_log_dt, "ev_b_re": ev_b_re, "ev_b_im": ev_b_im,
        "ev_c_re": ev_c_re, "ev_c_im": ev_c_im, "ev_d": ev_d,
        "ev_w_glu": ev_w_glu, "ev_b_glu": ev_b_glu, "ev_conv_w": ev_conv_w, "ev_w_out": ev_w_out,
        "od_w_in": od_w_in, "od_rel_bias": od_rel_bias, "od_pool_w": od_pool_w,
        "od_pool_scale": od_pool_scale, "od_w_out": od_w_out,
        "ln_mix_g": ln_mix_g, "ln_mix_b": ln_mix_b, "ln_ffn_g": ln_ffn_g, "ln_ffn_b": ln_ffn_b,
        "ffn_w_up": ffn_w_up, "ffn_w_down": ffn_w_down,
        "ple_w_proj": ple_w_proj, "ple_w_gate": ple_w_gate, "ple_b_gate": ple_b_gate,
    }


def reference(x, p, ev_w_in, ev_lambda_re, ev_lambda_im, ev_log_dt, ev_b_re, ev_b_im,
              ev_c_re, ev_c_im, ev_d, ev_w_glu, ev_b_glu, ev_conv_w, ev_w_out,
              od_w_in, od_rel_bias, od_pool_w, od_pool_scale, od_w_out,
              ln_mix_g, ln_mix_b, ln_ffn_g, ln_ffn_b, ffn_w_up, ffn_w_down,
              ple_w_proj, ple_w_gate, ple_b_gate):
    for i in range(DEPTH):
        if i % 2 == 0:
            e = i // 2
            h = x @ ev_w_in[e]
            u_a, b_g, c_g, x_b = jnp.split(h, [MIX_A, MIX_A + MIX_B, MIX_A + 2 * MIX_B], axis=-1)
            y_a = s5_mixer(u_a, ev_lambda_re[e], ev_lambda_im[e], ev_log_dt[e], ev_b_re[e], ev_b_im[e],
                           ev_c_re[e], ev_c_im[e], ev_d[e], ev_w_glu[e], ev_b_glu[e])
            y_b = short_conv_mixer(b_g, c_g, x_b, ev_conv_w[e])
            mix = jnp.concatenate([y_a, y_b], axis=-1) @ ev_w_out[e]
        else:
            o = i // 2
            h = x @ od_w_in[o]
            q, k, v, z = jnp.split(h, [MIX_C, 2 * MIX_C, 3 * MIX_C], axis=-1)
            y_c = chunk_attention(q, k, v, od_rel_bias[o])
            y_d = pool_mixer(z, od_pool_w[o], od_pool_scale[o])
            mix = jnp.concatenate([y_c, y_d], axis=-1) @ od_w_out[o]
        x = layer_norm(ALPHA * x + mix, ln_mix_g[i], ln_mix_b[i])
        x = layer_norm(ALPHA * x + swiglu(x, ffn_w_up[i], ffn_w_down[i]), ln_ffn_g[i], ln_ffn_b[i])
        x = x + jax.nn.sigmoid(x @ ple_w_gate[i] + ple_b_gate[i]) * (p[i] @ ple_w_proj[i])
    return x
```

```python
import functools
import math

import jax
import jax.numpy as jnp
from jax import lax
from jax.experimental import pallas as pl
from jax.experimental.pallas import tpu as pltpu

F32 = jnp.float32
BF16 = jnp.bfloat16

D_MODEL = 1024
SEQ = 16384
DEPTH = 4
CHUNK = 64
MIX = D_MODEL // 2
S5_GROUP = 16
S5_GROUPS = MIX // S5_GROUP
S5_STATE = 64
S5_LANES = S5_GROUPS * S5_STATE
CONV_W = 3
ATT_HEADS = 8
HEAD_DIM = MIX // ATT_HEADS
LEFT_CHUNKS = 8
BAND = (LEFT_CHUNKS + 1) * CHUNK
MAX_REL = 128
POOL_WINDOWS = (2, 4, 8, 16)
POOL_GROUP = MIX // len(POOL_WINDOWS)
D_FF = ((math.ceil(8 * D_MODEL / 3) + 255) // 256) * 256
D_PLE = 256
ALPHA = (2 * DEPTH) ** 0.25
LN_EPS = 1e-5
NEG_INF = -1e30

SUBLANES = 8
LANES = 128
VMEM_LIMIT_BYTES = 56 * 1024 * 1024

TM_MIX = 256
TM_POST = 512
FF_CHUNK = 256
S5_BLOCK_IN = LANES
S5_BLOCKS = MIX // S5_BLOCK_IN
S5_BLOCK_STATE = S5_LANES // S5_BLOCKS
SCAN_SLAB = 512
ATT_QB = 2 * CHUNK
ATT_HIST = LEFT_CHUNKS * CHUNK
ATT_KB = ATT_HIST + ATT_QB
POOL_HIST = 16
CONV_HIST = SUBLANES


def _dot(a, b):
    return jnp.dot(a, b, preferred_element_type=F32)


def _layer_norm(v, g, b):
    mu = jnp.mean(v, axis=-1, keepdims=True)
    d = v - mu
    var = jnp.mean(d * d, axis=-1, keepdims=True)
    return d * lax.rsqrt(var + LN_EPS) * g + b


def _const_spec(shape):
    zeros = (0,) * len(shape)
    return pl.BlockSpec(shape, lambda i: zeros, pipeline_mode=pl.Buffered(1))


def _row_spec(tm, width):
    return pl.BlockSpec((tm, width), lambda i: (i, 0))


def _params():
    return pltpu.CompilerParams(dimension_semantics=("arbitrary",),
                                vmem_limit_bytes=VMEM_LIMIT_BYTES)


def _post_kernel(x_ref, y_ref, p_ref, wout_ref, g1_ref, b1_ref, wup_ref, wdn_ref,
                 g2_ref, b2_ref, wg_ref, bg_ref, wp_ref, o_ref):
    mix = _dot(y_ref[...], wout_ref[...])
    x1 = _layer_norm(ALPHA * x_ref[...] + mix, g1_ref[...], b1_ref[...])
    x1b = x1.astype(BF16)
    acc = jnp.zeros(x1.shape, F32)
    for c in range(D_FF // FF_CHUNK):
        lo = c * FF_CHUNK
        gate = _dot(x1b, wup_ref[:, lo:lo + FF_CHUNK])
        up = _dot(x1b, wup_ref[:, D_FF + lo:D_FF + lo + FF_CHUNK])
        act = (jax.nn.silu(gate) * up).astype(BF16)
        acc = acc + _dot(act, wdn_ref[lo:lo + FF_CHUNK, :])
    x2 = _layer_norm(ALPHA * x1 + acc, g2_ref[...], b2_ref[...])
    gate = jax.nn.sigmoid(_dot(x2.astype(BF16), wg_ref[...]) + bg_ref[...])
    emb = _dot(p_ref[...].astype(BF16), wp_ref[...])
    o_ref[...] = x2 + gate * emb


def _post(x, y, p, wout, g1, b1, wup, wdn, g2, b2, wg, bg, wp):
    tm = TM_POST
    return pl.pallas_call(
        _post_kernel,
        grid=(SEQ // tm,),
        in_specs=[
            _row_spec(tm, D_MODEL), _row_spec(tm, D_MODEL), _row_spec(tm, D_PLE),
            _const_spec((D_MODEL, D_MODEL)), _const_spec((1, D_MODEL)), _const_spec((1, D_MODEL)),
            _const_spec((D_MODEL, 2 * D_FF)), _const_spec((D_FF, D_MODEL)),
            _const_spec((1, D_MODEL)), _const_spec((1, D_MODEL)),
            _const_spec((D_MODEL, D_MODEL)), _const_spec((1, D_MODEL)),
            _const_spec((D_PLE, D_MODEL)),
        ],
        out_specs=_row_spec(tm, D_MODEL),
        out_shape=jax.ShapeDtypeStruct((SEQ, D_MODEL), F32),
        compiler_params=_params(),
        name="post",
    )(x, y, p, wout, g1, b1, wup, wdn, g2, b2, wg, bg, wp)


def _cmul(ar, ai, br, bi):
    return ar * br - ai * bi, ar * bi + ai * br


def _s5_setup(lam_re_ref, lam_im_ref, logdt_ref, braw_re_ref, braw_im_ref,
              tab_ref, bsc_re_ref, bsc_im_ref):
    lre = lam_re_ref[...]
    lim = lam_im_ref[...]
    dt = jnp.exp(logdt_ref[...])
    mag = jnp.exp(lre * dt)
    ang = lim * dt
    lb_re = mag * jnp.cos(ang)
    lb_im = mag * jnp.sin(ang)
    den = lre * lre + lim * lim
    nr = lb_re - 1.0
    ni = lb_im
    r_re = (nr * lre + ni * lim) / den
    r_im = (ni * lre - nr * lim) / den
    for j in range(S5_BLOCKS):
        cols = slice(j * S5_BLOCK_STATE, (j + 1) * S5_BLOCK_STATE)
        b_re = braw_re_ref[j]
        b_im = braw_im_ref[j]
        bsc_re_ref[j] = (r_re[:, cols] * b_re - r_im[:, cols] * b_im).astype(BF16)
        bsc_im_ref[j] = (r_re[:, cols] * b_im + r_im[:, cols] * b_re).astype(BF16)
    powers = [(lb_re, lb_im)]
    for _ in range(SUBLANES - 1):
        powers.append(_cmul(powers[-1][0], powers[-1][1], lb_re, lb_im))
    row = lax.broadcasted_iota(jnp.int32, (SUBLANES, S5_LANES), 0)
    zero = jnp.zeros((SUBLANES, S5_LANES), F32)
    for t, d in enumerate((1, 2, 4)):
        pr, pi = powers[d - 1]
        tab_ref[2 * t] = jnp.where(row >= d, jnp.broadcast_to(pr, zero.shape), zero)
        tab_ref[2 * t + 1] = jnp.where(row >= d, jnp.broadcast_to(pi, zero.shape), zero)
    car_re = zero
    car_im = zero
    for k in range(SUBLANES):
        car_re = jnp.where(row == k, jnp.broadcast_to(powers[k][0], zero.shape), car_re)
        car_im = jnp.where(row == k, jnp.broadcast_to(powers[k][1], zero.shape), car_im)
    tab_ref[6] = car_re
    tab_ref[7] = car_im


def _s5_scan(tm, tab_ref, bu_re_ref, bu_im_ref, carry_re_ref, carry_im_ref):
    for s in range(S5_LANES // SCAN_SLAB):
        cols = slice(s * SCAN_SLAB, (s + 1) * SCAN_SLAB)
        steps = [(tab_ref[2 * t, :, cols], tab_ref[2 * t + 1, :, cols], d)
                 for t, d in enumerate((1, 2, 4))]
        pw_re = tab_ref[6, :, cols]
        pw_im = tab_ref[7, :, cols]

        def body(b, carry, cols=cols, steps=steps, pw_re=pw_re, pw_im=pw_im):
            c_re, c_im = carry
            rows = pl.ds(pl.multiple_of(b * SUBLANES, SUBLANES), SUBLANES)
            re = bu_re_ref[rows, cols]
            im = bu_im_ref[rows, cols]
            for a_re, a_im, d in steps:
                s_re = pltpu.roll(re, d, 0)
                s_im = pltpu.roll(im, d, 0)
                m_re, m_im = _cmul(a_re, a_im, s_re, s_im)
                re = re + m_re
                im = im + m_im
            m_re, m_im = _cmul(pw_re, pw_im, c_re, c_im)
            re = re + m_re
            im = im + m_im
            bu_re_ref[rows, cols] = re
            bu_im_ref[rows, cols] = im
            last = slice(SUBLANES - 1, SUBLANES)
            return (jnp.broadcast_to(re[last, :], re.shape),
                    jnp.broadcast_to(im[last, :], im.shape))

        c_re, c_im = lax.fori_loop(0, tm // SUBLANES, body,
                                   (carry_re_ref[:, cols], carry_im_ref[:, cols]))
        carry_re_ref[:, cols] = c_re
        carry_im_ref[:, cols] = c_im


def _even_kernel(x_ref, win_ref, lam_re_ref, lam_im_ref, logdt_ref, braw_re_ref, braw_im_ref,
                 c_re_ref, c_im_ref, dskip_ref, wglu_ref, bglu_ref, convw_ref, o_ref,
                 tab_ref, bsc_re_ref, bsc_im_ref, bu_re_ref, bu_im_ref,
                 carry_re_ref, carry_im_ref, zbuf_ref):
    tm = x_ref.shape[0]

    @pl.when(pl.program_id(0) == 0)
    def _():
        _s5_setup(lam_re_ref, lam_im_ref, logdt_ref, braw_re_ref, braw_im_ref,
                  tab_ref, bsc_re_ref, bsc_im_ref)
        carry_re_ref[...] = jnp.zeros(carry_re_ref.shape, F32)
        carry_im_ref[...] = jnp.zeros(carry_im_ref.shape, F32)
        zbuf_ref[0:CONV_HIST, :] = jnp.zeros((CONV_HIST, MIX), F32)

    h = _dot(x_ref[...].astype(BF16), win_ref[...])
    u = h[:, 0:MIX]

    ub = u.astype(BF16)
    for j in range(S5_BLOCKS):
        uj = ub[:, j * S5_BLOCK_IN:(j + 1) * S5_BLOCK_IN]
        cols = slice(j * S5_BLOCK_STATE, (j + 1) * S5_BLOCK_STATE)
        bu_re_ref[:, cols] = _dot(uj, bsc_re_ref[j])
        bu_im_ref[:, cols] = _dot(uj, bsc_im_ref[j])
    _s5_scan(tm, tab_ref, bu_re_ref, bu_im_ref, carry_re_ref, carry_im_ref)
    ys = []
    for j in range(S5_BLOCKS):
        cols = slice(j * S5_BLOCK_STATE, (j + 1) * S5_BLOCK_STATE)
        ys.append(_dot(bu_re_ref[:, cols].astype(BF16), c_re_ref[j])
                  - _dot(bu_im_ref[:, cols].astype(BF16), c_im_ref[j]))
    y = jnp.concatenate(ys, axis=-1) + dskip_ref[...] * u
    g = jax.nn.gelu(y)
    out_a = g * jax.nn.sigmoid(_dot(g.astype(BF16), wglu_ref[...]) + bglu_ref[...])
    o_ref[:, 0:MIX] = out_a.astype(o_ref.dtype)

    z = h[:, 2 * MIX:3 * MIX] * h[:, 3 * MIX:4 * MIX]
    zbuf_ref[CONV_HIST:CONV_HIST + tm, :] = z
    conv = convw_ref[CONV_W - 1:CONV_W, :] * z
    for k in range(CONV_W - 1):
        lag = CONV_W - 1 - k
        conv = conv + convw_ref[k:k + 1, :] * zbuf_ref[CONV_HIST - lag:CONV_HIST - lag + tm, :]
    o_ref[:, MIX:2 * MIX] = (h[:, MIX:2 * MIX] * conv).astype(o_ref.dtype)
    zbuf_ref[0:CONV_HIST, :] = zbuf_ref[tm:tm + CONV_HIST, :]


def _even_mixer(x, win, lam_re, lam_im, logdt, braw_re, braw_im, c_re, c_im,
                dskip, wglu, bglu, convw):
    tm = TM_MIX
    blk_b = (S5_BLOCKS, S5_BLOCK_IN, S5_BLOCK_STATE)
    blk_c = (S5_BLOCKS, S5_BLOCK_STATE, S5_BLOCK_IN)
    return pl.pallas_call(
        _even_kernel,
        grid=(SEQ // tm,),
        in_specs=[
            _row_spec(tm, D_MODEL), _const_spec((D_MODEL, 4 * MIX)),
            _const_spec((1, S5_LANES)), _const_spec((1, S5_LANES)), _const_spec((1, S5_LANES)),
            _const_spec(blk_b), _const_spec(blk_b), _const_spec(blk_c), _const_spec(blk_c),
            _const_spec((1, MIX)), _const_spec((MIX, MIX)), _const_spec((1, MIX)),
            _const_spec((CONV_W, MIX)),
        ],
        out_specs=_row_spec(tm, D_MODEL),
        out_shape=jax.ShapeDtypeStruct((SEQ, D_MODEL), BF16),
        scratch_shapes=[
            pltpu.VMEM((8, SUBLANES, S5_LANES), F32),
            pltpu.VMEM(blk_b, BF16), pltpu.VMEM(blk_b, BF16),
            pltpu.VMEM((tm, S5_LANES), F32), pltpu.VMEM((tm, S5_LANES), F32),
            pltpu.VMEM((SUBLANES, S5_LANES), F32), pltpu.VMEM((SUBLANES, S5_LANES), F32),
            pltpu.VMEM((CONV_HIST + tm, MIX), F32),
        ],
        compiler_params=_params(),
        name="even_mixer",
    )(x, win, lam_re, lam_im, logdt, braw_re, braw_im, c_re, c_im, dskip, wglu, bglu, convw)


def _odd_kernel(x_ref, win_ref, bias_ref, poolw_ref, pscale_ref, o_ref,
                k_ref, v_ref, zbuf_ref):
    tm = x_ref.shape[0]
    i = pl.program_id(0)

    @pl.when(i == 0)
    def _():
        k_ref[0:ATT_HIST, :] = jnp.zeros((ATT_HIST, MIX), BF16)
        v_ref[0:ATT_HIST, :] = jnp.zeros((ATT_HIST, MIX), BF16)
        zbuf_ref[0:POOL_HIST, :] = jnp.zeros((POOL_HIST, MIX), F32)

    h = _dot(x_ref[...].astype(BF16), win_ref[...])
    q = (h[:, 0:MIX] * (HEAD_DIM ** -0.5)).astype(BF16)
    k_ref[ATT_HIST:ATT_HIST + tm, :] = h[:, MIX:2 * MIX].astype(BF16)
    v_ref[ATT_HIST:ATT_HIST + tm, :] = h[:, 2 * MIX:3 * MIX].astype(BF16)

    lane = lax.broadcasted_iota(jnp.int32, (ATT_QB, LANES), 1)
    key_col = lax.broadcasted_iota(jnp.int32, (ATT_QB, ATT_KB), 1)
    for qb in range(tm // ATT_QB):
        r0 = qb * ATT_QB
        first_key = i * tm - ATT_HIST + r0
        started = key_col + first_key >= 0
        for j in range(MIX // LANES):
            cols = slice(j * LANES, (j + 1) * LANES)
            q2 = q[r0:r0 + ATT_QB, cols]
            k2 = k_ref[r0:r0 + ATT_KB, cols]
            v2 = v_ref[r0:r0 + ATT_KB, cols]
            halves = []
            for half in range(2):
                in_head = (lane >= half * HEAD_DIM) & (lane < (half + 1) * HEAD_DIM)
                qm = jnp.where(in_head, q2, jnp.zeros_like(q2))
                s = lax.dot_general(qm, k2, (((1,), (1,)), ((), ())),
                                    preferred_element_type=F32)
                s = jnp.where(started, s + bias_ref[2 * j + half], NEG_INF)
                m = jnp.max(s, axis=-1, keepdims=True)
                e = jnp.exp(s - m)
                denom = jnp.sum(e, axis=-1, keepdims=True)
                halves.append(_dot(e.astype(BF16), v2) * (1.0 / denom))
            o2 = jnp.where(lane < HEAD_DIM, halves[0], halves[1])
            o_ref[r0:r0 + ATT_QB, cols] = o2.astype(o_ref.dtype)
    k_ref[0:ATT_HIST, :] = k_ref[tm:tm + ATT_HIST, :]
    v_ref[0:ATT_HIST, :] = v_ref[tm:tm + ATT_HIST, :]

    z = h[:, 3 * MIX:4 * MIX]
    zbuf_ref[POOL_HIST:POOL_HIST + tm, :] = z
    t_pos = i * tm + lax.broadcasted_iota(jnp.int32, (tm, 1), 0)
    mixed = []
    for gi, w in enumerate(POOL_WINDOWS):
        cols = slice(gi * POOL_GROUP, (gi + 1) * POOL_GROUP)
        zg = z[:, cols]
        total = zg
        for lag in range(1, w):
            total = total + zbuf_ref[POOL_HIST - lag:POOL_HIST - lag + tm, cols]
        count = jnp.minimum(t_pos + 1, w).astype(F32)
        pooled = total / count - zg
        mixed.append(_dot(pooled.astype(BF16), poolw_ref[gi]))
    y_d = jnp.concatenate(mixed, axis=-1) * pscale_ref[...]
    o_ref[:, MIX:2 * MIX] = y_d.astype(o_ref.dtype)
    zbuf_ref[0:POOL_HIST, :] = zbuf_ref[tm:tm + POOL_HIST, :]


def _odd_mixer(x, win, bias, poolw, pscale):
    tm = TM_MIX
    return pl.pallas_call(
        _odd_kernel,
        grid=(SEQ // tm,),
        in_specs=[
            _row_spec(tm, D_MODEL), _const_spec((D_MODEL, 4 * MIX)),
            _const_spec((ATT_HEADS, ATT_QB, ATT_KB)),
            _const_spec((len(POOL_WINDOWS), POOL_GROUP, POOL_GROUP)),
            _const_spec((1, MIX)),
        ],
        out_specs=_row_spec(tm, D_MODEL),
        out_shape=jax.ShapeDtypeStruct((SEQ, D_MODEL), BF16),
        scratch_shapes=[
            pltpu.VMEM((ATT_HIST + tm, MIX), BF16), pltpu.VMEM((ATT_HIST + tm, MIX), BF16),
            pltpu.VMEM((POOL_HIST + tm, MIX), F32),
        ],
        compiler_params=_params(),
        name="odd_mixer",
    )(x, win, bias, poolw, pscale)


def _block_diag(w):
    per = S5_GROUPS // S5_BLOCKS
    a, b = w.shape[1], w.shape[2]
    w = w.reshape(S5_BLOCKS, per, a, b)
    eye = jnp.eye(per, dtype=w.dtype)
    out = w[:, :, :, None, :] * eye[None, :, None, :, None]
    return out.reshape(S5_BLOCKS, per * a, per * b)


def _band_bias(rel_bias):
    qi = jnp.arange(CHUNK)[:, None]
    kj = jnp.arange(BAND)[None, :]
    rel = jnp.clip(qi + LEFT_CHUNKS * CHUNK - kj, -MAX_REL, MAX_REL) + MAX_REL
    compact = rel_bias.astype(F32)[:, rel]
    rows = []
    for cq in range(ATT_QB // CHUNK):
        left = jnp.full((ATT_HEADS, CHUNK, cq * CHUNK), NEG_INF, F32)
        right = jnp.full((ATT_HEADS, CHUNK, ATT_KB - BAND - cq * CHUNK), NEG_INF, F32)
        rows.append(jnp.concatenate([left, compact, right], axis=-1))
    return jnp.concatenate(rows, axis=1)


def kernel(x, p, ev_w_in, ev_lambda_re, ev_lambda_im, ev_log_dt, ev_b_re, ev_b_im, ev_c_re, ev_c_im, ev_d, ev_w_glu, ev_b_glu, ev_conv_w, ev_w_out, od_w_in, od_rel_bias, od_pool_w, od_pool_scale, od_w_out, ln_mix_g, ln_mix_b, ln_ffn_g, ln_ffn_b, ffn_w_up, ffn_w_down, ple_w_proj, ple_w_gate, ple_b_gate):
    assert x.shape == (1, SEQ, D_MODEL) and p.shape == (DEPTH, 1, SEQ, D_PLE)
    xs = x.reshape(SEQ, D_MODEL)
    row = lambda v: v.reshape(1, -1)
    for i in range(DEPTH):
        if i % 2 == 0:
            e = i // 2
            y = _even_mixer(
                xs, ev_w_in[e].astype(BF16),
                row(ev_lambda_re[e]), row(ev_lambda_im[e]),
                row(jnp.repeat(ev_log_dt[e], S5_STATE)),
                _block_diag(jnp.swapaxes(ev_b_re[e], 1, 2)),
                _block_diag(jnp.swapaxes(ev_b_im[e], 1, 2)),
                _block_diag(jnp.swapaxes(ev_c_re[e], 1, 2)).astype(BF16),
                _block_diag(jnp.swapaxes(ev_c_im[e], 1, 2)).astype(BF16),
                row(ev_d[e]), ev_w_glu[e].astype(BF16), row(ev_b_glu[e]), ev_conv_w[e])
            w_out = ev_w_out[e]
        else:
            o = i // 2
            y = _odd_mixer(xs, od_w_in[o].astype(BF16), _band_bias(od_rel_bias[o]),
                           od_pool_w[o].astype(BF16), row(od_pool_scale[o]))
            w_out = od_w_out[o]
        xs = _post(xs, y, p[i, 0], w_out.astype(BF16), row(ln_mix_g[i]), row(ln_mix_b[i]),
                   ffn_w_up[i].astype(BF16), ffn_w_down[i].astype(BF16),
                   row(ln_ffn_g[i]), row(ln_ffn_b[i]),
                   ple_w_gate[i].astype(BF16), row(ple_b_gate[i]), ple_w_proj[i].astype(BF16))
    return xs.reshape(1, SEQ, D_MODEL)
```

```python
import functools
import math

import jax
import jax.numpy as jnp
from jax import lax
from jax.experimental import pallas as pl
from jax.experimental.pallas import tpu as pltpu

F32 = jnp.float32
BF16 = jnp.bfloat16

D_MODEL = 1024
SEQ = 16384
DEPTH = 4
CHUNK = 64
MIX = D_MODEL // 2
S5_GROUP = 16
S5_GROUPS = MIX // S5_GROUP
S5_STATE = 64
S5_LANES = S5_GROUPS * S5_STATE
CONV_W = 3
ATT_HEADS = 8
HEAD_DIM = MIX // ATT_HEADS
LEFT_CHUNKS = 8
BAND = (LEFT_CHUNKS + 1) * CHUNK
MAX_REL = 128
POOL_WINDOWS = (2, 4, 8, 16)
POOL_GROUP = MIX // len(POOL_WINDOWS)
D_FF = ((math.ceil(8 * D_MODEL / 3) + 255) // 256) * 256
D_PLE = 256
ALPHA = (2 * DEPTH) ** 0.25
LN_EPS = 1e-5
NEG_INF = -1e30

SUBLANES = 8
LANES = 128
VMEM_LIMIT_BYTES = 56 * 1024 * 1024

TM_MIX = 256
TM_POST = 512
FF_CHUNK = 256
S5_BLOCK_IN = LANES
S5_BLOCKS = MIX // S5_BLOCK_IN
S5_BLOCK_STATE = S5_LANES // S5_BLOCKS
SCAN_SLAB = 512
ATT_QB = 2 * CHUNK
ATT_HIST = LEFT_CHUNKS * CHUNK
ATT_KB = ATT_HIST + ATT_QB
POOL_HIST = 16
CONV_HIST = SUBLANES


def _dot(a, b):
    return jnp.dot(a, b, preferred_element_type=F32)


def _layer_norm(v, g, b):
    mu = jnp.mean(v, axis=-1, keepdims=True)
    d = v - mu
    var = jnp.mean(d * d, axis=-1, keepdims=True)
    return d * lax.rsqrt(var + LN_EPS) * g + b


def _const_spec(shape):
    zeros = (0,) * len(shape)
    return pl.BlockSpec(shape, lambda i: zeros, pipeline_mode=pl.Buffered(1))


def _row_spec(tm, width):
    return pl.BlockSpec((tm, width), lambda i: (i, 0))


def _params():
    return pltpu.CompilerParams(dimension_semantics=("arbitrary",),
                                vmem_limit_bytes=VMEM_LIMIT_BYTES)


def _post_kernel(x_ref, y_ref, p_ref, wout_ref, g1_ref, b1_ref, wup_ref, wdn_ref,
                 g2_ref, b2_ref, wg_ref, bg_ref, wp_ref, o_ref):
    mix = _dot(y_ref[...], wout_ref[...])
    x1 = _layer_norm(ALPHA * x_ref[...] + mix, g1_ref[...], b1_ref[...])
    x1b = x1.astype(BF16)
    acc = jnp.zeros(x1.shape, F32)
    for c in range(D_FF // FF_CHUNK):
        lo = c * FF_CHUNK
        gate = _dot(x1b, wup_ref[:, lo:lo + FF_CHUNK])
        up = _dot(x1b, wup_ref[:, D_FF + lo:D_FF + lo + FF_CHUNK])
        act = (jax.nn.silu(gate) * up).astype(BF16)
        acc = acc + _dot(act, wdn_ref[lo:lo + FF_CHUNK, :])
    x2 = _layer_norm(ALPHA * x1 + acc, g2_ref[...], b2_ref[...])
    gate = jax.nn.sigmoid(_dot(x2.astype(BF16), wg_ref[...]) + bg_ref[...])
    emb = _dot(p_ref[...].astype(BF16), wp_ref[...])
    o_ref[...] = x2 + gate * emb


def _post(layer, x, y, p, wout, g1, b1, wup, wdn, g2, b2, wg, bg, wp):
    tm = TM_POST
    return pl.pallas_call(
        _post_kernel,
        grid=(SEQ // tm,),
        in_specs=[
            _row_spec(tm, D_MODEL), _row_spec(tm, D_MODEL),
            pl.BlockSpec((None, tm, D_PLE), lambda i: (layer, i, 0)),
            _const_spec((D_MODEL, D_MODEL)), _const_spec((1, D_MODEL)), _const_spec((1, D_MODEL)),
            _const_spec((D_MODEL, 2 * D_FF)), _const_spec((D_FF, D_MODEL)),
            _const_spec((1, D_MODEL)), _const_spec((1, D_MODEL)),
            _const_spec((D_MODEL, D_MODEL)), _const_spec((1, D_MODEL)),
            _const_spec((D_PLE, D_MODEL)),
        ],
        out_specs=_row_spec(tm, D_MODEL),
        out_shape=jax.ShapeDtypeStruct((SEQ, D_MODEL), F32),
        compiler_params=_params(),
        name="post",
    )(x, y, p, wout, g1, b1, wup, wdn, g2, b2, wg, bg, wp)


def _cmul(ar, ai, br, bi):
    return ar * br - ai * bi, ar * bi + ai * br


def _s5_setup(lam_re_ref, lam_im_ref, logdt_ref, braw_re_ref, braw_im_ref,
              tab_ref, bsc_re_ref, bsc_im_ref):
    lre = lam_re_ref[...]
    lim = lam_im_ref[...]
    dt = jnp.exp(logdt_ref[...])
    mag = jnp.exp(lre * dt)
    ang = lim * dt
    lb_re = mag * jnp.cos(ang)
    lb_im = mag * jnp.sin(ang)
    den = lre * lre + lim * lim
    nr = lb_re - 1.0
    ni = lb_im
    r_re = (nr * lre + ni * lim) / den
    r_im = (ni * lre - nr * lim) / den
    for j in range(S5_BLOCKS):
        cols = slice(j * S5_BLOCK_STATE, (j + 1) * S5_BLOCK_STATE)
        b_re = braw_re_ref[j]
        b_im = braw_im_ref[j]
        bsc_re_ref[j] = (r_re[:, cols] * b_re - r_im[:, cols] * b_im).astype(BF16)
        bsc_im_ref[j] = (r_re[:, cols] * b_im + r_im[:, cols] * b_re).astype(BF16)
    powers = [(lb_re, lb_im)]
    for _ in range(SUBLANES - 1):
        powers.append(_cmul(powers[-1][0], powers[-1][1], lb_re, lb_im))
    row = lax.broadcasted_iota(jnp.int32, (SUBLANES, S5_LANES), 0)
    zero = jnp.zeros((SUBLANES, S5_LANES), F32)
    for t, d in enumerate((1, 2, 4)):
        pr, pi = powers[d - 1]
        tab_ref[2 * t] = jnp.where(row >= d, jnp.broadcast_to(pr, zero.shape), zero)
        tab_ref[2 * t + 1] = jnp.where(row >= d, jnp.broadcast_to(pi, zero.shape), zero)
    car_re = zero
    car_im = zero
    for k in range(SUBLANES):
        car_re = jnp.where(row == k, jnp.broadcast_to(powers[k][0], zero.shape), car_re)
        car_im = jnp.where(row == k, jnp.broadcast_to(powers[k][1], zero.shape), car_im)
    tab_ref[6] = car_re
    tab_ref[7] = car_im


def _s5_scan(tm, tab_ref, bu_re_ref, bu_im_ref, carry_re_ref, carry_im_ref):
    for s in range(S5_LANES // SCAN_SLAB):
        cols = slice(s * SCAN_SLAB, (s + 1) * SCAN_SLAB)
        steps = [(tab_ref[2 * t, :, cols], tab_ref[2 * t + 1, :, cols], d)
                 for t, d in enumerate((1, 2, 4))]
        pw_re = tab_ref[6, :, cols]
        pw_im = tab_ref[7, :, cols]

        def body(b, carry, cols=cols, steps=steps, pw_re=pw_re, pw_im=pw_im):
            c_re, c_im = carry
            rows = pl.ds(pl.multiple_of(b * SUBLANES, SUBLANES), SUBLANES)
            re = bu_re_ref[rows, cols]
            im = bu_im_ref[rows, cols]
            for a_re, a_im, d in steps:
                s_re = pltpu.roll(re, d, 0)
                s_im = pltpu.roll(im, d, 0)
                m_re, m_im = _cmul(a_re, a_im, s_re, s_im)
                re = re + m_re
                im = im + m_im
            m_re, m_im = _cmul(pw_re, pw_im, c_re, c_im)
            re = re + m_re
            im = im + m_im
            bu_re_ref[rows, cols] = re
            bu_im_ref[rows, cols] = im
            last = slice(SUBLANES - 1, SUBLANES)
            return (jnp.broadcast_to(re[last, :], re.shape),
                    jnp.broadcast_to(im[last, :], im.shape))

        c_re, c_im = lax.fori_loop(0, tm // SUBLANES, body,
                                   (carry_re_ref[:, cols], carry_im_ref[:, cols]))
        carry_re_ref[:, cols] = c_re
        carry_im_ref[:, cols] = c_im


def _even_kernel(x_ref, win_ref, lam_re_ref, lam_im_ref, logdt_ref, braw_re_ref, braw_im_ref,
                 c_re_ref, c_im_ref, dskip_ref, wglu_ref, bglu_ref, convw_ref, o_ref,
                 tab_ref, bsc_re_ref, bsc_im_ref, bu_re_ref, bu_im_ref,
                 carry_re_ref, carry_im_ref, zbuf_ref):
    tm = x_ref.shape[0]

    @pl.when(pl.program_id(0) == 0)
    def _():
        _s5_setup(lam_re_ref, lam_im_ref, logdt_ref, braw_re_ref, braw_im_ref,
                  tab_ref, bsc_re_ref, bsc_im_ref)
        carry_re_ref[...] = jnp.zeros(carry_re_ref.shape, F32)
        carry_im_ref[...] = jnp.zeros(carry_im_ref.shape, F32)
        zbuf_ref[0:CONV_HIST, :] = jnp.zeros((CONV_HIST, MIX), F32)

    h = _dot(x_ref[...].astype(BF16), win_ref[...])
    u = h[:, 0:MIX]

    ub = u.astype(BF16)
    for j in range(S5_BLOCKS):
        uj = ub[:, j * S5_BLOCK_IN:(j + 1) * S5_BLOCK_IN]
        cols = slice(j * S5_BLOCK_STATE, (j + 1) * S5_BLOCK_STATE)
        bu_re_ref[:, cols] = _dot(uj, bsc_re_ref[j])
        bu_im_ref[:, cols] = _dot(uj, bsc_im_ref[j])
    _s5_scan(tm, tab_ref, bu_re_ref, bu_im_ref, carry_re_ref, carry_im_ref)
    ys = []
    for j in range(S5_BLOCKS):
        cols = slice(j * S5_BLOCK_STATE, (j + 1) * S5_BLOCK_STATE)
        ys.append(_dot(bu_re_ref[:, cols].astype(BF16), c_re_ref[j])
                  - _dot(bu_im_ref[:, cols].astype(BF16), c_im_ref[j]))
    y = jnp.concatenate(ys, axis=-1) + dskip_ref[...] * u
    g = jax.nn.gelu(y)
    out_a = g * jax.nn.sigmoid(_dot(g.astype(BF16), wglu_ref[...]) + bglu_ref[...])
    o_ref[:, 0:MIX] = out_a.astype(o_ref.dtype)

    z = h[:, 2 * MIX:3 * MIX] * h[:, 3 * MIX:4 * MIX]
    zbuf_ref[CONV_HIST:CONV_HIST + tm, :] = z
    conv = convw_ref[CONV_W - 1:CONV_W, :] * z
    for k in range(CONV_W - 1):
        lag = CONV_W - 1 - k
        conv = conv + convw_ref[k:k + 1, :] * zbuf_ref[CONV_HIST - lag:CONV_HIST - lag + tm, :]
    o_ref[:, MIX:2 * MIX] = (h[:, MIX:2 * MIX] * conv).astype(o_ref.dtype)
    zbuf_ref[0:CONV_HIST, :] = zbuf_ref[tm:tm + CONV_HIST, :]


def _even_mixer(x, win, lam_re, lam_im, logdt, braw_re, braw_im, c_re, c_im,
                dskip, wglu, bglu, convw):
    tm = TM_MIX
    blk_b = (S5_BLOCKS, S5_BLOCK_IN, S5_BLOCK_STATE)
    blk_c = (S5_BLOCKS, S5_BLOCK_STATE, S5_BLOCK_IN)
    return pl.pallas_call(
        _even_kernel,
        grid=(SEQ // tm,),
        in_specs=[
            _row_spec(tm, D_MODEL), _const_spec((D_MODEL, 4 * MIX)),
            _const_spec((1, S5_LANES)), _const_spec((1, S5_LANES)), _const_spec((1, S5_LANES)),
            _const_spec(blk_b), _const_spec(blk_b), _const_spec(blk_c), _const_spec(blk_c),
            _const_spec((1, MIX)), _const_spec((MIX, MIX)), _const_spec((1, MIX)),
            _const_spec((CONV_W, MIX)),
        ],
        out_specs=_row_spec(tm, D_MODEL),
        out_shape=jax.ShapeDtypeStruct((SEQ, D_MODEL), BF16),
        scratch_shapes=[
            pltpu.VMEM((8, SUBLANES, S5_LANES), F32),
            pltpu.VMEM(blk_b, BF16), pltpu.VMEM(blk_b, BF16),
            pltpu.VMEM((tm, S5_LANES), F32), pltpu.VMEM((tm, S5_LANES), F32),
            pltpu.VMEM((SUBLANES, S5_LANES), F32), pltpu.VMEM((SUBLANES, S5_LANES), F32),
            pltpu.VMEM((CONV_HIST + tm, MIX), F32),
        ],
        compiler_params=_params(),
        name="even_mixer",
    )(x, win, lam_re, lam_im, logdt, braw_re, braw_im, c_re, c_im, dskip, wglu, bglu, convw)


def _odd_kernel(x_ref, win_ref, bias_ref, poolw_ref, pscale_ref, o_ref,
                k_ref, v_ref, zbuf_ref):
    tm = x_ref.shape[0]
    i = pl.program_id(0)

    @pl.when(i == 0)
    def _():
        k_ref[0:ATT_HIST, :] = jnp.zeros((ATT_HIST, MIX), BF16)
        v_ref[0:ATT_HIST, :] = jnp.zeros((ATT_HIST, MIX), BF16)
        zbuf_ref[0:POOL_HIST, :] = jnp.zeros((POOL_HIST, MIX), F32)

    h = _dot(x_ref[...].astype(BF16), win_ref[...])
    q = (h[:, 0:MIX] * (HEAD_DIM ** -0.5)).astype(BF16)
    k_ref[ATT_HIST:ATT_HIST + tm, :] = h[:, MIX:2 * MIX].astype(BF16)
    v_ref[ATT_HIST:ATT_HIST + tm, :] = h[:, 2 * MIX:3 * MIX].astype(BF16)

    lane = lax.broadcasted_iota(jnp.int32, (ATT_QB, LANES), 1)
    key_col = lax.broadcasted_iota(jnp.int32, (ATT_QB, ATT_KB), 1)
    for qb in range(tm // ATT_QB):
        r0 = qb * ATT_QB
        first_key = i * tm - ATT_HIST + r0
        started = key_col + first_key >= 0
        for j in range(MIX // LANES):
            cols = slice(j * LANES, (j + 1) * LANES)
            q2 = q[r0:r0 + ATT_QB, cols]
            k2 = k_ref[r0:r0 + ATT_KB, cols]
            v2 = v_ref[r0:r0 + ATT_KB, cols]
            halves = []
            for half in range(2):
                in_head = (lane >= half * HEAD_DIM) & (lane < (half + 1) * HEAD_DIM)
                qm = jnp.where(in_head, q2, jnp.zeros_like(q2))
                s = lax.dot_general(qm, k2, (((1,), (1,)), ((), ())),
                                    preferred_element_type=F32)
                s = jnp.where(started, s + bias_ref[2 * j + half], NEG_INF)
                m = jnp.max(s, axis=-1, keepdims=True)
                e = jnp.exp(s - m)
                denom = jnp.sum(e, axis=-1, keepdims=True)
                halves.append(_dot(e.astype(BF16), v2) * (1.0 / denom))
            o2 = jnp.where(lane < HEAD_DIM, halves[0], halves[1])
            o_ref[r0:r0 + ATT_QB, cols] = o2.astype(o_ref.dtype)
    k_ref[0:ATT_HIST, :] = k_ref[tm:tm + ATT_HIST, :]
    v_ref[0:ATT_HIST, :] = v_ref[tm:tm + ATT_HIST, :]

    z = h[:, 3 * MIX:4 * MIX]
    zbuf_ref[POOL_HIST:POOL_HIST + tm, :] = z
    t_pos = i * tm + lax.broadcasted_iota(jnp.int32, (tm, 1), 0)
    mixed = []
    for gi, w in enumerate(POOL_WINDOWS):
        cols = slice(gi * POOL_GROUP, (gi + 1) * POOL_GROUP)
        zg = z[:, cols]
        total = zg
        for lag in range(1, w):
            total = total + zbuf_ref[POOL_HIST - lag:POOL_HIST - lag + tm, cols]
        count = jnp.minimum(t_pos + 1, w).astype(F32)
        pooled = total / count - zg
        mixed.append(_dot(pooled.astype(BF16), poolw_ref[gi]))
    y_d = jnp.concatenate(mixed, axis=-1) * pscale_ref[...]
    o_ref[:, MIX:2 * MIX] = y_d.astype(o_ref.dtype)
    zbuf_ref[0:POOL_HIST, :] = zbuf_ref[tm:tm + POOL_HIST, :]


def _odd_mixer(x, win, bias, poolw, pscale):
    tm = TM_MIX
    return pl.pallas_call(
        _odd_kernel,
        grid=(SEQ // tm,),
        in_specs=[
            _row_spec(tm, D_MODEL), _const_spec((D_MODEL, 4 * MIX)),
            _const_spec((ATT_HEADS, ATT_QB, ATT_KB)),
            _const_spec((len(POOL_WINDOWS), POOL_GROUP, POOL_GROUP)),
            _const_spec((1, MIX)),
        ],
        out_specs=_row_spec(tm, D_MODEL),
        out_shape=jax.ShapeDtypeStruct((SEQ, D_MODEL), BF16),
        scratch_shapes=[
            pltpu.VMEM((ATT_HIST + tm, MIX), BF16), pltpu.VMEM((ATT_HIST + tm, MIX), BF16),
            pltpu.VMEM((POOL_HIST + tm, MIX), F32),
        ],
        compiler_params=_params(),
        name="odd_mixer",
    )(x, win, bias, poolw, pscale)


def _block_diag(w):
    per = S5_GROUPS // S5_BLOCKS
    a, b = w.shape[1], w.shape[2]
    w = w.reshape(S5_BLOCKS, per, a, b)
    eye = jnp.eye(per, dtype=w.dtype)
    out = w[:, :, :, None, :] * eye[None, :, None, :, None]
    return out.reshape(S5_BLOCKS, per * a, per * b)


def _band_bias(rel_bias):
    assert CHUNK - 1 <= MAX_REL <= LEFT_CHUNKS * CHUNK
    n_far = CHUNK - 1 + LEFT_CHUNKS * CHUNK - MAX_REL
    rev = rel_bias.astype(F32)[:, ::-1]
    by_offset = jnp.concatenate(
        [jnp.broadcast_to(rev[:, :1], (ATT_HEADS, n_far)), rev[:, :BAND + CHUNK - 1 - n_far]], axis=-1)
    compact = jnp.stack([by_offset[:, CHUNK - 1 - qi:CHUNK - 1 - qi + BAND] for qi in range(CHUNK)],
                        axis=1)
    rows = []
    for cq in range(ATT_QB // CHUNK):
        left = jnp.full((ATT_HEADS, CHUNK, cq * CHUNK), NEG_INF, F32)
        right = jnp.full((ATT_HEADS, CHUNK, ATT_KB - BAND - cq * CHUNK), NEG_INF, F32)
        rows.append(jnp.concatenate([left, compact, right], axis=-1))
    return jnp.concatenate(rows, axis=1)


def kernel(x, p, ev_w_in, ev_lambda_re, ev_lambda_im, ev_log_dt, ev_b_re, ev_b_im, ev_c_re, ev_c_im, ev_d, ev_w_glu, ev_b_glu, ev_conv_w, ev_w_out, od_w_in, od_rel_bias, od_pool_w, od_pool_scale, od_w_out, ln_mix_g, ln_mix_b, ln_ffn_g, ln_ffn_b, ffn_w_up, ffn_w_down, ple_w_proj, ple_w_gate, ple_b_gate):
    assert x.shape == (1, SEQ, D_MODEL) and p.shape == (DEPTH, 1, SEQ, D_PLE)
    xs = x.reshape(SEQ, D_MODEL)
    ps = p.reshape(DEPTH, SEQ, D_PLE)
    row = lambda v: v.reshape(1, -1)
    for i in range(DEPTH):
        if i % 2 == 0:
            e = i // 2
            y = _even_mixer(
                xs, ev_w_in[e].astype(BF16),
                row(ev_lambda_re[e]), row(ev_lambda_im[e]),
                row(jnp.repeat(ev_log_dt[e], S5_STATE)),
                _block_diag(jnp.swapaxes(ev_b_re[e], 1, 2)),
                _block_diag(jnp.swapaxes(ev_b_im[e], 1, 2)),
                _block_diag(jnp.swapaxes(ev_c_re[e], 1, 2)).astype(BF16),
                _block_diag(jnp.swapaxes(ev_c_im[e], 1, 2)).astype(BF16),
                row(ev_d[e]), ev_w_glu[e].astype(BF16), row(ev_b_glu[e]), ev_conv_w[e])
            w_out = ev_w_out[e]
        else:
            o = i // 2
            y = _odd_mixer(xs, od_w_in[o].astype(BF16), _band_bias(od_rel_bias[o]),
                           od_pool_w[o].astype(BF16), row(od_pool_scale[o]))
            w_out = od_w_out[o]
        xs = _post(i, xs, y, ps, w_out.astype(BF16), row(ln_mix_g[i]), row(ln_mix_b[i]),
                   ffn_w_up[i].astype(BF16), ffn_w_down[i].astype(BF16),
                   row(ln_ffn_g[i]), row(ln_ffn_b[i]),
                   ple_w_gate[i].astype(BF16), row(ple_b_gate[i]), ple_w_proj[i].astype(BF16))
    return xs.reshape(1, SEQ, D_MODEL)
```

```python
import math

import jax
import jax.numpy as jnp
from jax import lax
from jax.experimental import pallas as pl
from jax.experimental.pallas import tpu as pltpu

F32 = jnp.float32
BF16 = jnp.bfloat16

D_MODEL = 1024
SEQ = 16384
DEPTH = 4
CHUNK = 64
MIX = D_MODEL // 2
S5_GROUP = 16
S5_GROUPS = MIX // S5_GROUP
S5_STATE = 64
S5_LANES = S5_GROUPS * S5_STATE
CONV_W = 3
ATT_HEADS = 8
HEAD_DIM = MIX // ATT_HEADS
LEFT_CHUNKS = 8
BAND = (LEFT_CHUNKS + 1) * CHUNK
MAX_REL = 128
POOL_WINDOWS = (2, 4, 8, 16)
POOL_GROUP = MIX // len(POOL_WINDOWS)
D_FF = ((math.ceil(8 * D_MODEL / 3) + 255) // 256) * 256
D_PLE = 256
ALPHA = (2 * DEPTH) ** 0.25
LN_EPS = 1e-5
NEG_INF = -1e30

SUBLANES = 8
LANES = 128
VMEM_LIMIT_BYTES = 58 * 1024 * 1024

TM = 256
N_TILES = SEQ // TM
FF_CHUNK = 256
N_FF = D_FF // FF_CHUNK
S5_BLOCK_IN = LANES
S5_BLOCKS = MIX // S5_BLOCK_IN
S5_BLOCK_STATE = S5_LANES // S5_BLOCKS
SCAN_BLOCKS = TM // SUBLANES
SCAN_SLAB = 512
SCAN_PIECE = 4
SCAN_DELAY = 2
ATT_QB = 2 * CHUNK
ATT_HIST = LEFT_CHUNKS * CHUNK
ATT_KB = ATT_HIST + ATT_QB
POOL_HIST = 16
CONV_HIST = SUBLANES


def _dot(a, b):
    return jnp.dot(a, b, preferred_element_type=F32)


def _layer_norm(v, g, b):
    mu = jnp.mean(v, axis=-1, keepdims=True)
    d = v - mu
    var = jnp.mean(d * d, axis=-1, keepdims=True)
    return d * lax.rsqrt(var + LN_EPS) * g + b


def _const_spec(shape):
    zeros = (0,) * len(shape)
    return pl.BlockSpec(shape, lambda i: zeros, pipeline_mode=pl.Buffered(1))


def _run_pieces(post_pieces, mixer_pieces):
    assert len(post_pieces) == len(mixer_pieces)
    for post_piece, mixer_piece in zip(post_pieces, mixer_pieces):
        post_piece()
        mixer_piece()


def _mixer_tile(i):
    return jnp.minimum(i, N_TILES - 1)


def _post_tile(i):
    return jnp.maximum(i - 1, 0)


POST_SPECS = [
    _const_spec((D_MODEL, D_MODEL)), _const_spec((1, D_MODEL)), _const_spec((1, D_MODEL)),
    _const_spec((D_MODEL, 2 * D_FF)), _const_spec((D_FF, D_MODEL)),
    _const_spec((1, D_MODEL)), _const_spec((1, D_MODEL)),
    _const_spec((D_MODEL, D_MODEL)), _const_spec((1, D_MODEL)),
    _const_spec((D_PLE, D_MODEL)),
]
POST_SCRATCH = [
    pltpu.VMEM((TM, D_MODEL), F32),
    pltpu.VMEM((TM, D_MODEL), BF16),
    pltpu.VMEM((2, TM, FF_CHUNK), BF16),
    pltpu.VMEM((TM, D_MODEL), F32),
]


def _post_pieces(x_ref, y_ref, p_ref, wout_ref, g1_ref, b1_ref, wup_ref, wdn_ref,
                 g2_ref, b2_ref, wg_ref, bg_ref, wp_ref, o_ref,
                 x1_ref, x1b_ref, act_ref, acc_ref):
    def head():
        mix = _dot(y_ref[...], wout_ref[...])
        x1 = _layer_norm(ALPHA * x_ref[...] + mix, g1_ref[...], b1_ref[...])
        x1_ref[...] = x1
        x1b_ref[...] = x1.astype(BF16)

    def down(c):
        part = _dot(act_ref[c % 2], wdn_ref[c * FF_CHUNK:(c + 1) * FF_CHUNK, :])
        acc_ref[...] = part if c == 0 else acc_ref[...] + part

    def swiglu(c):
        lo = c * FF_CHUNK
        x1b = x1b_ref[...]
        gate = _dot(x1b, wup_ref[:, lo:lo + FF_CHUNK])
        up = _dot(x1b, wup_ref[:, D_FF + lo:D_FF + lo + FF_CHUNK])
        if c > 0:
            down(c - 1)
        act_ref[c % 2] = (jax.nn.silu(gate) * up).astype(BF16)

    def tail():
        down(N_FF - 1)
        x2 = _layer_norm(ALPHA * x1_ref[...] + acc_ref[...], g2_ref[...], b2_ref[...])
        gate = jax.nn.sigmoid(_dot(x2.astype(BF16), wg_ref[...]) + bg_ref[...])
        emb = _dot(p_ref[...].astype(BF16), wp_ref[...])
        o_ref[...] = x2 + gate * emb

    return [head] + [lambda c=c: swiglu(c) for c in range(N_FF)] + [tail]


def _cmul(ar, ai, br, bi):
    return ar * br - ai * bi, ar * bi + ai * br


def _s5_setup(lam_re_ref, lam_im_ref, logdt_ref, braw_re_ref, braw_im_ref,
              tab_ref, bsc_re_ref, bsc_im_ref):
    lre = lam_re_ref[...]
    lim = lam_im_ref[...]
    dt = jnp.exp(logdt_ref[...])
    mag = jnp.exp(lre * dt)
    ang = lim * dt
    lb_re = mag * jnp.cos(ang)
    lb_im = mag * jnp.sin(ang)
    den = lre * lre + lim * lim
    nr = lb_re - 1.0
    ni = lb_im
    r_re = (nr * lre + ni * lim) / den
    r_im = (ni * lre - nr * lim) / den
    for j in range(S5_BLOCKS):
        cols = slice(j * S5_BLOCK_STATE, (j + 1) * S5_BLOCK_STATE)
        b_re = braw_re_ref[j]
        b_im = braw_im_ref[j]
        bsc_re_ref[j] = (r_re[:, cols] * b_re - r_im[:, cols] * b_im).astype(BF16)
        bsc_im_ref[j] = (r_re[:, cols] * b_im + r_im[:, cols] * b_re).astype(BF16)
    powers = [(lb_re, lb_im)]
    for _ in range(SUBLANES - 1):
        powers.append(_cmul(powers[-1][0], powers[-1][1], lb_re, lb_im))
    row = lax.broadcasted_iota(jnp.int32, (SUBLANES, S5_LANES), 0)
    zero = jnp.zeros((SUBLANES, S5_LANES), F32)
    for t, d in enumerate((1, 2, 4)):
        pr, pi = powers[d - 1]
        tab_ref[2 * t] = jnp.where(row >= d, jnp.broadcast_to(pr, zero.shape), zero)
        tab_ref[2 * t + 1] = jnp.where(row >= d, jnp.broadcast_to(pi, zero.shape), zero)
    car_re = zero
    car_im = zero
    for k in range(SUBLANES):
        car_re = jnp.where(row == k, jnp.broadcast_to(powers[k][0], zero.shape), car_re)
        car_im = jnp.where(row == k, jnp.broadcast_to(powers[k][1], zero.shape), car_im)
    tab_ref[6] = car_re
    tab_ref[7] = car_im


def _s5_scan_blocks(first, last, tab_ref, bu_re_ref, bu_im_ref, carry_re_ref, carry_im_ref):
    base = jnp.minimum(pl.program_id(0), 0) * SUBLANES
    for s in range(S5_LANES // SCAN_SLAB):
        cols = slice(s * SCAN_SLAB, (s + 1) * SCAN_SLAB)
        mult = [(tab_ref[2 * t, :, cols], tab_ref[2 * t + 1, :, cols]) for t in range(3)]
        pw_re = tab_ref[6, :, cols]
        pw_im = tab_ref[7, :, cols]
        c_re = carry_re_ref[:, cols]
        c_im = carry_im_ref[:, cols]
        for b in range(first, last):
            rows = pl.ds(pl.multiple_of(base + b * SUBLANES, SUBLANES), SUBLANES)
            re = bu_re_ref[rows, cols]
            im = bu_im_ref[rows, cols]
            for (a_re, a_im), d in zip(mult, (1, 2, 4)):
                m_re, m_im = _cmul(a_re, a_im, pltpu.roll(re, d, 0), pltpu.roll(im, d, 0))
                re = re + m_re
                im = im + m_im
            m_re, m_im = _cmul(pw_re, pw_im, c_re, c_im)
            re = re + m_re
            im = im + m_im
            bu_re_ref[rows, cols] = re
            bu_im_ref[rows, cols] = im
            top = slice(SUBLANES - 1, SUBLANES)
            c_re = jnp.broadcast_to(re[top, :], re.shape)
            c_im = jnp.broadcast_to(im[top, :], im.shape)
        carry_re_ref[:, cols] = c_re
        carry_im_ref[:, cols] = c_im


def _even_mixer_pieces(x_ref, win_ref, bsc_re_ref, bsc_im_ref, c_re_ref, c_im_ref, dskip_ref,
                       wglu_ref, bglu_ref, convw_ref, tab_ref, bu_re_ref, bu_im_ref,
                       carry_re_ref, carry_im_ref, zbuf_ref, u_ref, y_ref):
    def head():
        h = _dot(x_ref[...].astype(BF16), win_ref[...])
        u = h[:, 0:MIX]
        u_ref[...] = u
        ub = u.astype(BF16)
        for j in range(S5_BLOCKS):
            uj = ub[:, j * S5_BLOCK_IN:(j + 1) * S5_BLOCK_IN]
            cols = slice(j * S5_BLOCK_STATE, (j + 1) * S5_BLOCK_STATE)
            bu_re_ref[:, cols] = _dot(uj, bsc_re_ref[j])
            bu_im_ref[:, cols] = _dot(uj, bsc_im_ref[j])
        z = h[:, 2 * MIX:3 * MIX] * h[:, 3 * MIX:4 * MIX]
        zbuf_ref[CONV_HIST:CONV_HIST + TM, :] = z
        conv = convw_ref[CONV_W - 1:CONV_W, :] * z
        for k in range(CONV_W - 1):
            lag = CONV_W - 1 - k
            conv = conv + convw_ref[k:k + 1, :] * zbuf_ref[CONV_HIST - lag:CONV_HIST - lag + TM, :]
        y_ref[:, MIX:2 * MIX] = (h[:, MIX:2 * MIX] * conv).astype(y_ref.dtype)
        zbuf_ref[0:CONV_HIST, :] = zbuf_ref[TM:TM + CONV_HIST, :]

    def scan(first):
        _s5_scan_blocks(first, min(first + SCAN_PIECE, SCAN_BLOCKS), tab_ref, bu_re_ref,
                        bu_im_ref, carry_re_ref, carry_im_ref)

    def tail():
        ys = []
        for j in range(S5_BLOCKS):
            cols = slice(j * S5_BLOCK_STATE, (j + 1) * S5_BLOCK_STATE)
            ys.append(_dot(bu_re_ref[:, cols].astype(BF16), c_re_ref[j])
                      - _dot(bu_im_ref[:, cols].astype(BF16), c_im_ref[j]))
        y = jnp.concatenate(ys, axis=-1) + dskip_ref[...] * u_ref[...]
        g = jax.nn.gelu(y)
        out_a = g * jax.nn.sigmoid(_dot(g.astype(BF16), wglu_ref[...]) + bglu_ref[...])
        y_ref[:, 0:MIX] = out_a.astype(y_ref.dtype)

    scans = [lambda f=f: scan(f) for f in range(0, SCAN_BLOCKS, SCAN_PIECE)]
    idle = lambda: None
    pieces = [head] + [idle] * SCAN_DELAY + scans
    return pieces + [idle] * (N_FF + 1 - len(pieces)) + [tail]


def _even_layer_kernel(x_ref, xp_ref, p_ref, win_ref, lam_re_ref, lam_im_ref, logdt_ref,
                       braw_re_ref, braw_im_ref, c_re_ref, c_im_ref, dskip_ref, wglu_ref,
                       bglu_ref, convw_ref, wout_ref, g1_ref, b1_ref, wup_ref, wdn_ref,
                       g2_ref, b2_ref, wg_ref, bg_ref, wp_ref, o_ref,
                       y_ref, x1_ref, x1b_ref, act_ref, acc_ref,
                       tab_ref, bsc_re_ref, bsc_im_ref, bu_re_ref, bu_im_ref,
                       carry_re_ref, carry_im_ref, zbuf_ref, u_ref):
    @pl.when(pl.program_id(0) == 0)
    def _():
        _s5_setup(lam_re_ref, lam_im_ref, logdt_ref, braw_re_ref, braw_im_ref,
                  tab_ref, bsc_re_ref, bsc_im_ref)
        carry_re_ref[...] = jnp.zeros(carry_re_ref.shape, F32)
        carry_im_ref[...] = jnp.zeros(carry_im_ref.shape, F32)
        zbuf_ref[0:CONV_HIST, :] = jnp.zeros((CONV_HIST, MIX), F32)
        y_ref[...] = jnp.zeros(y_ref.shape, y_ref.dtype)

    post = _post_pieces(xp_ref, y_ref, p_ref, wout_ref, g1_ref, b1_ref, wup_ref, wdn_ref,
                        g2_ref, b2_ref, wg_ref, bg_ref, wp_ref, o_ref,
                        x1_ref, x1b_ref, act_ref, acc_ref)
    mixer = _even_mixer_pieces(x_ref, win_ref, bsc_re_ref, bsc_im_ref, c_re_ref, c_im_ref,
                               dskip_ref, wglu_ref, bglu_ref, convw_ref, tab_ref, bu_re_ref,
                               bu_im_ref, carry_re_ref, carry_im_ref, zbuf_ref, u_ref, y_ref)
    _run_pieces(post, mixer)


def _layer_call(kernel_fn, name, layer, x, p, mixer_args, mixer_specs, post_args, scratch):
    return pl.pallas_call(
        kernel_fn,
        grid=(N_TILES + 1,),
        in_specs=[
            pl.BlockSpec((TM, D_MODEL), lambda i: (_mixer_tile(i), 0)),
            pl.BlockSpec((TM, D_MODEL), lambda i: (_post_tile(i), 0)),
            pl.BlockSpec((None, TM, D_PLE), lambda i: (layer, _post_tile(i), 0)),
        ] + mixer_specs + POST_SPECS,
        out_specs=pl.BlockSpec((TM, D_MODEL), lambda i: (_post_tile(i), 0)),
        out_shape=jax.ShapeDtypeStruct((SEQ, D_MODEL), F32),
        scratch_shapes=[pltpu.VMEM((TM, D_MODEL), BF16)] + POST_SCRATCH + scratch,
        compiler_params=pltpu.CompilerParams(dimension_semantics=("arbitrary",),
                                             vmem_limit_bytes=VMEM_LIMIT_BYTES),
        name=name,
    )(x, x, p, *mixer_args, *post_args)


def _even_layer(layer, x, p, mixer_args, post_args):
    blk_b = (S5_BLOCKS, S5_BLOCK_IN, S5_BLOCK_STATE)
    blk_c = (S5_BLOCKS, S5_BLOCK_STATE, S5_BLOCK_IN)
    mixer_specs = [
        _const_spec((D_MODEL, 4 * MIX)),
        _const_spec((1, S5_LANES)), _const_spec((1, S5_LANES)), _const_spec((1, S5_LANES)),
        _const_spec(blk_b), _const_spec(blk_b), _const_spec(blk_c), _const_spec(blk_c),
        _const_spec((1, MIX)), _const_spec((MIX, MIX)), _const_spec((1, MIX)),
        _const_spec((CONV_W, MIX)),
    ]
    scratch = [
        pltpu.VMEM((8, SUBLANES, S5_LANES), F32),
        pltpu.VMEM(blk_b, BF16), pltpu.VMEM(blk_b, BF16),
        pltpu.VMEM((TM, S5_LANES), F32), pltpu.VMEM((TM, S5_LANES), F32),
        pltpu.VMEM((SUBLANES, S5_LANES), F32), pltpu.VMEM((SUBLANES, S5_LANES), F32),
        pltpu.VMEM((CONV_HIST + TM, MIX), F32),
        pltpu.VMEM((TM, MIX), F32),
    ]
    return _layer_call(_even_layer_kernel, "even_layer", layer, x, p, mixer_args, mixer_specs,
                       post_args, scratch)


def _odd_mixer_pieces(x_ref, win_ref, bias_ref, poolw_ref, pscale_ref, k_ref, v_ref, zbuf_ref,
                      q_ref, e_ref, r_ref, y_ref):
    i = pl.program_id(0)

    def head():
        h = _dot(x_ref[...].astype(BF16), win_ref[...])
        q_ref[...] = (h[:, 0:MIX] * (HEAD_DIM ** -0.5)).astype(BF16)
        k_ref[ATT_HIST:ATT_HIST + TM, :] = h[:, MIX:2 * MIX].astype(BF16)
        v_ref[ATT_HIST:ATT_HIST + TM, :] = h[:, 2 * MIX:3 * MIX].astype(BF16)
        z = h[:, 3 * MIX:4 * MIX]
        zbuf_ref[POOL_HIST:POOL_HIST + TM, :] = z
        t_pos = i * TM + lax.broadcasted_iota(jnp.int32, (TM, 1), 0)
        mixed = []
        for gi, w in enumerate(POOL_WINDOWS):
            cols = slice(gi * POOL_GROUP, (gi + 1) * POOL_GROUP)
            zg = z[:, cols]
            total = zg
            for lag in range(1, w):
                total = total + zbuf_ref[POOL_HIST - lag:POOL_HIST - lag + TM, cols]
            count = jnp.minimum(t_pos + 1, w).astype(F32)
            pooled = total / count - zg
            mixed.append(_dot(pooled.astype(BF16), poolw_ref[gi]))
        y_d = jnp.concatenate(mixed, axis=-1) * pscale_ref[...]
        y_ref[:, MIX:2 * MIX] = y_d.astype(y_ref.dtype)
        zbuf_ref[0:POOL_HIST, :] = zbuf_ref[TM:TM + POOL_HIST, :]

    units = [(qb, j) for qb in range(TM // ATT_QB) for j in range(MIX // LANES)]
    lane = lambda: lax.broadcasted_iota(jnp.int32, (ATT_QB, LANES), 1)

    def scores(n):
        qb, j = units[n]
        r0 = qb * ATT_QB
        first_key = i * TM - ATT_HIST + r0
        key_col = lax.broadcasted_iota(jnp.int32, (ATT_QB, ATT_KB), 1)
        started = key_col + first_key >= 0
        cols = slice(j * LANES, (j + 1) * LANES)
        q2 = q_ref[r0:r0 + ATT_QB, cols]
        k2 = k_ref[r0:r0 + ATT_KB, cols]
        for half in range(2):
            in_head = (lane() >= half * HEAD_DIM) & (lane() < (half + 1) * HEAD_DIM)
            qm = jnp.where(in_head, q2, jnp.zeros_like(q2))
            s = lax.dot_general(qm, k2, (((1,), (1,)), ((), ())), preferred_element_type=F32)
            s = jnp.where(started, s + bias_ref[2 * j + half], NEG_INF)
            m = jnp.max(s, axis=-1, keepdims=True)
            e = jnp.exp(s - m)
            denom = jnp.sum(e, axis=-1, keepdims=True)
            e_ref[n % 2, half] = e.astype(BF16)
            r_ref[n % 2, half] = jnp.broadcast_to(1.0 / denom, (ATT_QB, LANES))

    def context(n):
        qb, j = units[n]
        r0 = qb * ATT_QB
        cols = slice(j * LANES, (j + 1) * LANES)
        v2 = v_ref[r0:r0 + ATT_KB, cols]
        halves = [_dot(e_ref[n % 2, half], v2) * r_ref[n % 2, half] for half in range(2)]
        o2 = jnp.where(lane() < HEAD_DIM, halves[0], halves[1])
        y_ref[r0:r0 + ATT_QB, cols] = o2.astype(y_ref.dtype)

    def attend(n):
        if n < len(units):
            scores(n)
        if 0 < n <= len(units):
            context(n - 1)

    def tail():
        k_ref[0:ATT_HIST, :] = k_ref[TM:TM + ATT_HIST, :]
        v_ref[0:ATT_HIST, :] = v_ref[TM:TM + ATT_HIST, :]

    assert len(units) < N_FF
    return [head] + [lambda n=n: attend(n) for n in range(N_FF)] + [tail]


def _odd_layer_kernel(x_ref, xp_ref, p_ref, win_ref, bias_ref, poolw_ref, pscale_ref,
                      wout_ref, g1_ref, b1_ref, wup_ref, wdn_ref, g2_ref, b2_ref, wg_ref,
                      bg_ref, wp_ref, o_ref, y_ref, x1_ref, x1b_ref, act_ref, acc_ref,
                      k_ref, v_ref, zbuf_ref, q_ref, e_ref, r_ref):
    @pl.when(pl.program_id(0) == 0)
    def _():
        k_ref[0:ATT_HIST, :] = jnp.zeros((ATT_HIST, MIX), BF16)
        v_ref[0:ATT_HIST, :] = jnp.zeros((ATT_HIST, MIX), BF16)
        zbuf_ref[0:POOL_HIST, :] = jnp.zeros((POOL_HIST, MIX), F32)
        y_ref[...] = jnp.zeros(y_ref.shape, y_ref.dtype)

    post = _post_pieces(xp_ref, y_ref, p_ref, wout_ref, g1_ref, b1_ref, wup_ref, wdn_ref,
                        g2_ref, b2_ref, wg_ref, bg_ref, wp_ref, o_ref,
                        x1_ref, x1b_ref, act_ref, acc_ref)
    mixer = _odd_mixer_pieces(x_ref, win_ref, bias_ref, poolw_ref, pscale_ref, k_ref, v_ref,
                              zbuf_ref, q_ref, e_ref, r_ref, y_ref)
    _run_pieces(post, mixer)


def _odd_layer(layer, x, p, mixer_args, post_args):
    mixer_specs = [
        _const_spec((D_MODEL, 4 * MIX)),
        _const_spec((ATT_HEADS, ATT_QB, ATT_KB)),
        _const_spec((len(POOL_WINDOWS), POOL_GROUP, POOL_GROUP)),
        _const_spec((1, MIX)),
    ]
    scratch = [
        pltpu.VMEM((ATT_HIST + TM, MIX), BF16), pltpu.VMEM((ATT_HIST + TM, MIX), BF16),
        pltpu.VMEM((POOL_HIST + TM, MIX), F32),
        pltpu.VMEM((TM, MIX), BF16),
        pltpu.VMEM((2, 2, ATT_QB, ATT_KB), BF16),
        pltpu.VMEM((2, 2, ATT_QB, LANES), F32),
    ]
    return _layer_call(_odd_layer_kernel, "odd_layer", layer, x, p, mixer_args, mixer_specs,
                       post_args, scratch)


def _block_diag(w):
    per = S5_GROUPS // S5_BLOCKS
    a, b = w.shape[1], w.shape[2]
    w = w.reshape(S5_BLOCKS, per, a, b)
    eye = jnp.eye(per, dtype=w.dtype)
    out = w[:, :, :, None, :] * eye[None, :, None, :, None]
    return out.reshape(S5_BLOCKS, per * a, per * b)


def _band_bias(rel_bias):
    assert CHUNK - 1 <= MAX_REL <= LEFT_CHUNKS * CHUNK
    n_far = CHUNK - 1 + LEFT_CHUNKS * CHUNK - MAX_REL
    rev = rel_bias.astype(F32)[:, ::-1]
    by_offset = jnp.concatenate(
        [jnp.broadcast_to(rev[:, :1], (ATT_HEADS, n_far)), rev[:, :BAND + CHUNK - 1 - n_far]], axis=-1)
    compact = jnp.stack([by_offset[:, CHUNK - 1 - qi:CHUNK - 1 - qi + BAND] for qi in range(CHUNK)],
                        axis=1)
    rows = []
    for cq in range(ATT_QB // CHUNK):
        left = jnp.full((ATT_HEADS, CHUNK, cq * CHUNK), NEG_INF, F32)
        right = jnp.full((ATT_HEADS, CHUNK, ATT_KB - BAND - cq * CHUNK), NEG_INF, F32)
        rows.append(jnp.concatenate([left, compact, right], axis=-1))
    return jnp.concatenate(rows, axis=1)


def kernel(x, p, ev_w_in, ev_lambda_re, ev_lambda_im, ev_log_dt, ev_b_re, ev_b_im, ev_c_re, ev_c_im, ev_d, ev_w_glu, ev_b_glu, ev_conv_w, ev_w_out, od_w_in, od_rel_bias, od_pool_w, od_pool_scale, od_w_out, ln_mix_g, ln_mix_b, ln_ffn_g, ln_ffn_b, ffn_w_up, ffn_w_down, ple_w_proj, ple_w_gate, ple_b_gate):
    assert x.shape == (1, SEQ, D_MODEL) and p.shape == (DEPTH, 1, SEQ, D_PLE)
    xs = x.reshape(SEQ, D_MODEL)
    ps = p.reshape(DEPTH, SEQ, D_PLE)
    row = lambda v: v.reshape(1, -1)
    for i in range(DEPTH):
        w_out = ev_w_out[i // 2] if i % 2 == 0 else od_w_out[i // 2]
        post_args = (w_out.astype(BF16), row(ln_mix_g[i]), row(ln_mix_b[i]),
                     ffn_w_up[i].astype(BF16), ffn_w_down[i].astype(BF16),
                     row(ln_ffn_g[i]), row(ln_ffn_b[i]),
                     ple_w_gate[i].astype(BF16), row(ple_b_gate[i]), ple_w_proj[i].astype(BF16))
        if i % 2 == 0:
            e = i // 2
            mixer_args = (
                ev_w_in[e].astype(BF16),
                row(ev_lambda_re[e]), row(ev_lambda_im[e]),
                row(jnp.repeat(ev_log_dt[e], S5_STATE)),
                _block_diag(jnp.swapaxes(ev_b_re[e], 1, 2)),
                _block_diag(jnp.swapaxes(ev_b_im[e], 1, 2)),
                _block_diag(jnp.swapaxes(ev_c_re[e], 1, 2)).astype(BF16),
                _block_diag(jnp.swapaxes(ev_c_im[e], 1, 2)).astype(BF16),
                row(ev_d[e]), ev_w_glu[e].astype(BF16), row(ev_b_glu[e]), ev_conv_w[e])
            xs = _even_layer(i, xs, ps, mixer_args, post_args)
        else:
            o = i // 2
            mixer_args = (od_w_in[o].astype(BF16), _band_bias(od_rel_bias[o]),
                          od_pool_w[o].astype(BF16), row(od_pool_scale[o]))
            xs = _odd_layer(i, xs, ps, mixer_args, post_args)
    return xs.reshape(1, SEQ, D_MODEL)
```

```python
import functools
import math

import jax
import jax.numpy as jnp
from jax import lax
from jax.experimental import pallas as pl
from jax.experimental.pallas import tpu as pltpu

F32 = jnp.float32
BF16 = jnp.bfloat16

D_MODEL = 1024
SEQ = 16384
DEPTH = 4
CHUNK = 64
MIX = D_MODEL // 2
S5_GROUP = 16
S5_GROUPS = MIX // S5_GROUP
S5_STATE = 64
S5_LANES = S5_GROUPS * S5_STATE
CONV_W = 3
ATT_HEADS = 8
HEAD_DIM = MIX // ATT_HEADS
LEFT_CHUNKS = 8
BAND = (LEFT_CHUNKS + 1) * CHUNK
MAX_REL = 128
POOL_WINDOWS = (2, 4, 8, 16)
POOL_GROUP = MIX // len(POOL_WINDOWS)
D_FF = ((math.ceil(8 * D_MODEL / 3) + 255) // 256) * 256
D_PLE = 256
ALPHA = (2 * DEPTH) ** 0.25
LN_EPS = 1e-5
NEG_INF = -1e30

SUBLANES = 8
LANES = 128
BF16_ROWS = 16
VMEM_LIMIT_BYTES = 58 * 1024 * 1024

TM = 256
N_TILES = SEQ // TM
FF_CHUNK = 256
N_FF = D_FF // FF_CHUNK
S5_BLOCK_IN = LANES
S5_BLOCKS = MIX // S5_BLOCK_IN
S5_BLOCK_STATE = S5_LANES // S5_BLOCKS
SCAN_BLOCKS = TM // SUBLANES
SCAN_SLAB = 512
SCAN_PIECE = 4
SCAN_DELAY = 2
ATT_DELAY = 3
ATT_QB = 2 * CHUNK
ATT_HIST = LEFT_CHUNKS * CHUNK
ATT_KB = ATT_HIST + ATT_QB
EVEN_MIXER_IN = 12
ODD_MIXER_IN = 4
POOL_HIST = 16
CONV_HIST = SUBLANES


def _dot(a, b):
    return jnp.dot(a, b, preferred_element_type=F32)


def _layer_norm(v, g, b):
    mu = jnp.mean(v, axis=-1, keepdims=True)
    d = v - mu
    var = jnp.mean(d * d, axis=-1, keepdims=True)
    return d * lax.rsqrt(var + LN_EPS) * g + b


def _const_spec(shape):
    zeros = (0,) * len(shape)
    return pl.BlockSpec(shape, lambda i: zeros, pipeline_mode=pl.Buffered(1))


def _run_pieces(post_pieces, mixer_pieces):
    assert len(post_pieces) == len(mixer_pieces)
    for post_piece, mixer_piece in zip(post_pieces, mixer_pieces):
        post_piece()
        mixer_piece()


def _mixer_tile(i):
    return jnp.minimum(i, N_TILES - 1)


def _post_tile(i):
    return jnp.maximum(i - 1, 0)


POST_SPECS = [
    _const_spec((D_MODEL, D_MODEL)), _const_spec((1, D_MODEL)), _const_spec((1, D_MODEL)),
    _const_spec((D_MODEL, 2 * D_FF)), _const_spec((D_FF, D_MODEL)),
    _const_spec((1, D_MODEL)), _const_spec((1, D_MODEL)),
    _const_spec((D_MODEL, D_MODEL)), _const_spec((1, D_MODEL)),
    _const_spec((D_PLE, D_MODEL)),
]
POST_SCRATCH = [
    pltpu.VMEM((TM, D_MODEL), F32),
    pltpu.VMEM((TM, D_MODEL), BF16),
    pltpu.VMEM((2, TM, FF_CHUNK), BF16),
    pltpu.VMEM((TM, D_MODEL), F32),
]


def _post_pieces(x_ref, y_ref, p_ref, wout_ref, g1_ref, b1_ref, wup_ref, wdn_ref,
                 g2_ref, b2_ref, wg_ref, bg_ref, wp_ref, o_ref,
                 x1_ref, x1b_ref, act_ref, acc_ref):
    def head():
        mix = _dot(y_ref[...], wout_ref[...])
        x1 = _layer_norm(ALPHA * x_ref[...] + mix, g1_ref[...], b1_ref[...])
        x1_ref[...] = x1
        x1b_ref[...] = x1.astype(BF16)

    def down(c):
        part = _dot(act_ref[c % 2], wdn_ref[c * FF_CHUNK:(c + 1) * FF_CHUNK, :])
        acc_ref[...] = part if c == 0 else acc_ref[...] + part

    def swiglu(c):
        lo = c * FF_CHUNK
        x1b = x1b_ref[...]
        gate = _dot(x1b, wup_ref[:, lo:lo + FF_CHUNK])
        up = _dot(x1b, wup_ref[:, D_FF + lo:D_FF + lo + FF_CHUNK])
        if c > 0:
            down(c - 1)
        act_ref[c % 2] = (jax.nn.silu(gate) * up).astype(BF16)

    def tail():
        down(N_FF - 1)
        emb = _dot(p_ref[...].astype(BF16), wp_ref[...])
        x2 = _layer_norm(ALPHA * x1_ref[...] + acc_ref[...], g2_ref[...], b2_ref[...])
        gate = jax.nn.sigmoid(_dot(x2.astype(BF16), wg_ref[...]) + bg_ref[...])
        o_ref[...] = x2 + gate * emb

    return [head] + [lambda c=c: swiglu(c) for c in range(N_FF)] + [tail]


def _cmul(ar, ai, br, bi):
    return ar * br - ai * bi, ar * bi + ai * br


def _s5_setup(lam_re_ref, lam_im_ref, logdt_ref, braw_re_ref, braw_im_ref,
              tab_ref, bsc_re_ref, bsc_im_ref):
    lre = lam_re_ref[...]
    lim = lam_im_ref[...]
    dt = jnp.exp(logdt_ref[...])
    mag = jnp.exp(lre * dt)
    ang = lim * dt
    lb_re = mag * jnp.cos(ang)
    lb_im = mag * jnp.sin(ang)
    den = lre * lre + lim * lim
    nr = lb_re - 1.0
    ni = lb_im
    r_re = (nr * lre + ni * lim) / den
    r_im = (ni * lre - nr * lim) / den
    for j in range(S5_BLOCKS):
        cols = slice(j * S5_BLOCK_STATE, (j + 1) * S5_BLOCK_STATE)
        b_re = braw_re_ref[j]
        b_im = braw_im_ref[j]
        bsc_re_ref[j] = (r_re[:, cols] * b_re - r_im[:, cols] * b_im).astype(BF16)
        bsc_im_ref[j] = (r_re[:, cols] * b_im + r_im[:, cols] * b_re).astype(BF16)
    powers = [(lb_re, lb_im)]
    for _ in range(SUBLANES - 1):
        powers.append(_cmul(powers[-1][0], powers[-1][1], lb_re, lb_im))
    row = lax.broadcasted_iota(jnp.int32, (SUBLANES, S5_LANES), 0)
    zero = jnp.zeros((SUBLANES, S5_LANES), F32)
    for t, d in enumerate((1, 2, 4)):
        pr, pi = powers[d - 1]
        tab_ref[2 * t] = jnp.where(row >= d, jnp.broadcast_to(pr, zero.shape), zero)
        tab_ref[2 * t + 1] = jnp.where(row >= d, jnp.broadcast_to(pi, zero.shape), zero)
    car_re = zero
    car_im = zero
    for k in range(SUBLANES):
        car_re = jnp.where(row == k, jnp.broadcast_to(powers[k][0], zero.shape), car_re)
        car_im = jnp.where(row == k, jnp.broadcast_to(powers[k][1], zero.shape), car_im)
    tab_ref[6] = car_re
    tab_ref[7] = car_im


def _s5_scan_blocks(first, last, tab_ref, bu_re_ref, bu_im_ref, carry_re_ref, carry_im_ref):
    base = jnp.minimum(pl.program_id(0), 0) * SUBLANES
    for s in range(S5_LANES // SCAN_SLAB):
        cols = slice(s * SCAN_SLAB, (s + 1) * SCAN_SLAB)
        mult = [(tab_ref[2 * t, :, cols], tab_ref[2 * t + 1, :, cols]) for t in range(3)]
        pw_re = tab_ref[6, :, cols]
        pw_im = tab_ref[7, :, cols]
        c_re = carry_re_ref[:, cols]
        c_im = carry_im_ref[:, cols]
        for b in range(first, last):
            rows = pl.ds(pl.multiple_of(base + b * SUBLANES, SUBLANES), SUBLANES)
            re = bu_re_ref[rows, cols]
            im = bu_im_ref[rows, cols]
            for (a_re, a_im), d in zip(mult, (1, 2, 4)):
                m_re, m_im = _cmul(a_re, a_im, pltpu.roll(re, d, 0), pltpu.roll(im, d, 0))
                re = re + m_re
                im = im + m_im
            m_re, m_im = _cmul(pw_re, pw_im, c_re, c_im)
            re = re + m_re
            im = im + m_im
            bu_re_ref[rows, cols] = re
            bu_im_ref[rows, cols] = im
            top = slice(SUBLANES - 1, SUBLANES)
            c_re = jnp.broadcast_to(re[top, :], re.shape)
            c_im = jnp.broadcast_to(im[top, :], im.shape)
        carry_re_ref[:, cols] = c_re
        carry_im_ref[:, cols] = c_im


def _even_mixer_pieces(x_ref, win_ref, bsc_re_ref, bsc_im_ref, c_re_ref, c_im_ref, dskip_ref,
                       wglu_ref, bglu_ref, convw_ref, tab_ref, bu_re_ref, bu_im_ref,
                       carry_re_ref, carry_im_ref, zbuf_ref, u_ref, y_ref):
    def head():
        h = _dot(x_ref[...].astype(BF16), win_ref[...])
        u = h[:, 0:MIX]
        u_ref[...] = u
        ub = u.astype(BF16)
        for j in range(S5_BLOCKS):
            uj = ub[:, j * S5_BLOCK_IN:(j + 1) * S5_BLOCK_IN]
            cols = slice(j * S5_BLOCK_STATE, (j + 1) * S5_BLOCK_STATE)
            bu_re_ref[:, cols] = _dot(uj, bsc_re_ref[j])
            bu_im_ref[:, cols] = _dot(uj, bsc_im_ref[j])
        z = h[:, 2 * MIX:3 * MIX] * h[:, 3 * MIX:4 * MIX]
        zbuf_ref[CONV_HIST:CONV_HIST + TM, :] = z
        conv = convw_ref[CONV_W - 1:CONV_W, :] * z
        for k in range(CONV_W - 1):
            lag = CONV_W - 1 - k
            conv = conv + convw_ref[k:k + 1, :] * zbuf_ref[CONV_HIST - lag:CONV_HIST - lag + TM, :]
        y_ref[:, MIX:2 * MIX] = (h[:, MIX:2 * MIX] * conv).astype(y_ref.dtype)
        zbuf_ref[0:CONV_HIST, :] = zbuf_ref[TM:TM + CONV_HIST, :]

    def scan(first):
        _s5_scan_blocks(first, min(first + SCAN_PIECE, SCAN_BLOCKS), tab_ref, bu_re_ref,
                        bu_im_ref, carry_re_ref, carry_im_ref)

    def tail():
        ys = []
        for j in range(S5_BLOCKS):
            cols = slice(j * S5_BLOCK_STATE, (j + 1) * S5_BLOCK_STATE)
            ys.append(_dot(bu_re_ref[:, cols].astype(BF16), c_re_ref[j])
                      - _dot(bu_im_ref[:, cols].astype(BF16), c_im_ref[j]))
        y = jnp.concatenate(ys, axis=-1) + dskip_ref[...] * u_ref[...]
        g = jax.nn.gelu(y)
        out_a = g * jax.nn.sigmoid(_dot(g.astype(BF16), wglu_ref[...]) + bglu_ref[...])
        y_ref[:, 0:MIX] = out_a.astype(y_ref.dtype)

    scans = [lambda f=f: scan(f) for f in range(0, SCAN_BLOCKS, SCAN_PIECE)]
    idle = lambda: None
    pieces = [head] + [idle] * SCAN_DELAY + scans
    return pieces + [idle] * (N_FF + 1 - len(pieces)) + [tail]


def _split_refs(refs, n_mixer_in, n_cast):
    refs = list(refs)
    sizes = (3, n_mixer_in, len(POST_SPECS), n_cast, 1, n_cast)
    groups = []
    for size in sizes:
        groups.append(refs[:size])
        refs = refs[size:]
    return groups + [refs]


def _cast_weights(cast_in, cast_out):
    for w_ref, o_ref in zip(cast_in, cast_out):
        o_ref[...] = w_ref[...].astype(BF16)


def _even_layer_kernel(n_cast, *refs):
    (x_ref, xp_ref, p_ref), mixer_in, post_in, cast_in, (o_ref,), cast_out, scratch = _split_refs(
        refs, EVEN_MIXER_IN, n_cast)
    (win_ref, lam_re_ref, lam_im_ref, logdt_ref, braw_re_ref, braw_im_ref, c_re_ref, c_im_ref,
     dskip_ref, wglu_ref, bglu_ref, convw_ref) = mixer_in
    wout_ref, g1_ref, b1_ref, wup_ref, wdn_ref, g2_ref, b2_ref, wg_ref, bg_ref, wp_ref = post_in
    (y_ref, x1_ref, x1b_ref, act_ref, acc_ref, tab_ref, bsc_re_ref, bsc_im_ref, bu_re_ref,
     bu_im_ref, carry_re_ref, carry_im_ref, zbuf_ref, u_ref) = scratch

    @pl.when(pl.program_id(0) == 0)
    def _():
        _s5_setup(lam_re_ref, lam_im_ref, logdt_ref, braw_re_ref, braw_im_ref,
                  tab_ref, bsc_re_ref, bsc_im_ref)
        carry_re_ref[...] = jnp.zeros(carry_re_ref.shape, F32)
        carry_im_ref[...] = jnp.zeros(carry_im_ref.shape, F32)
        zbuf_ref[0:CONV_HIST, :] = jnp.zeros((CONV_HIST, MIX), F32)
        y_ref[...] = jnp.zeros(y_ref.shape, y_ref.dtype)

    _cast_weights(cast_in, cast_out)

    post = _post_pieces(xp_ref, y_ref, p_ref, wout_ref, g1_ref, b1_ref, wup_ref, wdn_ref,
                        g2_ref, b2_ref, wg_ref, bg_ref, wp_ref, o_ref,
                        x1_ref, x1b_ref, act_ref, acc_ref)
    mixer = _even_mixer_pieces(x_ref, win_ref, bsc_re_ref, bsc_im_ref, c_re_ref, c_im_ref,
                               dskip_ref, wglu_ref, bglu_ref, convw_ref, tab_ref, bu_re_ref,
                               bu_im_ref, carry_re_ref, carry_im_ref, zbuf_ref, u_ref, y_ref)
    _run_pieces(post, mixer)


def _cast_specs(casts):
    in_specs, out_specs, out_shapes = [], [], []
    for stacked, index in casts:
        _, rows, cols = stacked.shape
        block = next(b for b in range(BF16_ROWS, rows + 1, BF16_ROWS)
                     if rows % b == 0 and rows // b <= N_TILES)
        last = rows // block - 1
        in_specs.append(pl.BlockSpec(
            (None, block, cols), lambda i, index=index, last=last: (index, jnp.minimum(i, last), 0)))
        out_specs.append(pl.BlockSpec(
            (block, cols), lambda i, last=last: (jnp.minimum(i, last), 0)))
        out_shapes.append(jax.ShapeDtypeStruct((rows, cols), BF16))
    return in_specs, out_specs, out_shapes


def _layer_call(kernel_fn, name, layer, x, p, mixer_args, mixer_specs, post_args, scratch,
                casts):
    cast_in_specs, cast_out_specs, cast_shapes = _cast_specs(casts)
    out, *cast = pl.pallas_call(
        functools.partial(kernel_fn, len(casts)),
        grid=(N_TILES + 1,),
        in_specs=[
            pl.BlockSpec((TM, D_MODEL), lambda i: (_mixer_tile(i), 0)),
            pl.BlockSpec((TM, D_MODEL), lambda i: (_post_tile(i), 0)),
            pl.BlockSpec((None, TM, D_PLE), lambda i: (layer, _post_tile(i), 0)),
        ] + mixer_specs + POST_SPECS + cast_in_specs,
        out_specs=[pl.BlockSpec((TM, D_MODEL), lambda i: (_post_tile(i), 0))] + cast_out_specs,
        out_shape=[jax.ShapeDtypeStruct((SEQ, D_MODEL), F32)] + cast_shapes,
        scratch_shapes=[pltpu.VMEM((TM, D_MODEL), BF16)] + POST_SCRATCH + scratch,
        compiler_params=pltpu.CompilerParams(dimension_semantics=("arbitrary",),
                                             vmem_limit_bytes=VMEM_LIMIT_BYTES),
        name=name,
    )(x, x, p, *mixer_args, *post_args, *[stacked for stacked, _ in casts])
    return out, cast


def _even_layer(layer, x, p, mixer_args, post_args, casts):
    blk_b = (S5_BLOCKS, S5_BLOCK_IN, S5_BLOCK_STATE)
    blk_c = (S5_BLOCKS, S5_BLOCK_STATE, S5_BLOCK_IN)
    mixer_specs = [
        _const_spec((D_MODEL, 4 * MIX)),
        _const_spec((1, S5_LANES)), _const_spec((1, S5_LANES)), _const_spec((1, S5_LANES)),
        _const_spec(blk_b), _const_spec(blk_b), _const_spec(blk_c), _const_spec(blk_c),
        _const_spec((1, MIX)), _const_spec((MIX, MIX)), _const_spec((1, MIX)),
        _const_spec((CONV_W, MIX)),
    ]
    scratch = [
        pltpu.VMEM((8, SUBLANES, S5_LANES), F32),
        pltpu.VMEM(blk_b, BF16), pltpu.VMEM(blk_b, BF16),
        pltpu.VMEM((TM, S5_LANES), F32), pltpu.VMEM((TM, S5_LANES), F32),
        pltpu.VMEM((SUBLANES, S5_LANES), F32), pltpu.VMEM((SUBLANES, S5_LANES), F32),
        pltpu.VMEM((CONV_HIST + TM, MIX), F32),
        pltpu.VMEM((TM, MIX), F32),
    ]
    assert len(mixer_specs) == EVEN_MIXER_IN
    return _layer_call(_even_layer_kernel, "even_layer", layer, x, p, mixer_args, mixer_specs,
                       post_args, scratch, casts)


def _odd_mixer_pieces(x_ref, win_ref, bias_ref, poolw_ref, pscale_ref, k_ref, v_ref, zbuf_ref,
                      q_ref, e_ref, r_ref, y_ref):
    i = pl.program_id(0)

    def head():
        h = _dot(x_ref[...].astype(BF16), win_ref[...])
        q_ref[...] = (h[:, 0:MIX] * (HEAD_DIM ** -0.5)).astype(BF16)
        k_ref[ATT_HIST:ATT_HIST + TM, :] = h[:, MIX:2 * MIX].astype(BF16)
        v_ref[ATT_HIST:ATT_HIST + TM, :] = h[:, 2 * MIX:3 * MIX].astype(BF16)
        z = h[:, 3 * MIX:4 * MIX]
        zbuf_ref[POOL_HIST:POOL_HIST + TM, :] = z
        t_pos = i * TM + lax.broadcasted_iota(jnp.int32, (TM, 1), 0)
        mixed = []
        for gi, w in enumerate(POOL_WINDOWS):
            cols = slice(gi * POOL_GROUP, (gi + 1) * POOL_GROUP)
            zg = z[:, cols]
            total = zg
            for lag in range(1, w):
                total = total + zbuf_ref[POOL_HIST - lag:POOL_HIST - lag + TM, cols]
            count = jnp.minimum(t_pos + 1, w).astype(F32)
            pooled = total / count - zg
            mixed.append(_dot(pooled.astype(BF16), poolw_ref[gi]))
        y_d = jnp.concatenate(mixed, axis=-1) * pscale_ref[...]
        y_ref[:, MIX:2 * MIX] = y_d.astype(y_ref.dtype)
        zbuf_ref[0:POOL_HIST, :] = zbuf_ref[TM:TM + POOL_HIST, :]

    units = [(qb, j) for qb in range(TM // ATT_QB) for j in range(MIX // LANES)]
    lane = lambda: lax.broadcasted_iota(jnp.int32, (ATT_QB, LANES), 1)

    def scores(n):
        qb, j = units[n]
        r0 = qb * ATT_QB
        first_key = i * TM - ATT_HIST + r0
        key_col = lax.broadcasted_iota(jnp.int32, (ATT_QB, ATT_KB), 1)
        started = key_col + first_key >= 0
        cols = slice(j * LANES, (j + 1) * LANES)
        q2 = q_ref[r0:r0 + ATT_QB, cols]
        k2 = k_ref[r0:r0 + ATT_KB, cols]
        for half in range(2):
            in_head = (lane() >= half * HEAD_DIM) & (lane() < (half + 1) * HEAD_DIM)
            qm = jnp.where(in_head, q2, jnp.zeros_like(q2))
            s = lax.dot_general(qm, k2, (((1,), (1,)), ((), ())), preferred_element_type=F32)
            s = jnp.where(started, s + bias_ref[2 * j + half], NEG_INF)
            m = jnp.max(s, axis=-1, keepdims=True)
            e = jnp.exp(s - m)
            denom = jnp.sum(e, axis=-1, keepdims=True)
            e_ref[n % 2, half] = e.astype(BF16)
            r_ref[n % 2, half] = jnp.broadcast_to(1.0 / denom, (ATT_QB, LANES))

    def context(n):
        qb, j = units[n]
        r0 = qb * ATT_QB
        cols = slice(j * LANES, (j + 1) * LANES)
        v2 = v_ref[r0:r0 + ATT_KB, cols]
        halves = [_dot(e_ref[n % 2, half], v2) * r_ref[n % 2, half] for half in range(2)]
        o2 = jnp.where(lane() < HEAD_DIM, halves[0], halves[1])
        y_ref[r0:r0 + ATT_QB, cols] = o2.astype(y_ref.dtype)

    def attend(n):
        if 0 <= n < len(units):
            scores(n)
        if 0 < n <= len(units):
            context(n - 1)

    def tail():
        attend(N_FF - ATT_DELAY)
        k_ref[0:ATT_HIST, :] = k_ref[TM:TM + ATT_HIST, :]
        v_ref[0:ATT_HIST, :] = v_ref[TM:TM + ATT_HIST, :]

    assert len(units) == N_FF - ATT_DELAY
    return [head] + [lambda k=k: attend(k - ATT_DELAY) for k in range(N_FF)] + [tail]


def _odd_layer_kernel(n_cast, *refs):
    (x_ref, xp_ref, p_ref), mixer_in, post_in, cast_in, (o_ref,), cast_out, scratch = _split_refs(
        refs, ODD_MIXER_IN, n_cast)
    win_ref, bias_ref, poolw_ref, pscale_ref = mixer_in
    wout_ref, g1_ref, b1_ref, wup_ref, wdn_ref, g2_ref, b2_ref, wg_ref, bg_ref, wp_ref = post_in
    (y_ref, x1_ref, x1b_ref, act_ref, acc_ref, k_ref, v_ref, zbuf_ref, q_ref, e_ref,
     r_ref) = scratch

    @pl.when(pl.program_id(0) == 0)
    def _():
        k_ref[0:ATT_HIST, :] = jnp.zeros((ATT_HIST, MIX), BF16)
        v_ref[0:ATT_HIST, :] = jnp.zeros((ATT_HIST, MIX), BF16)
        zbuf_ref[0:POOL_HIST, :] = jnp.zeros((POOL_HIST, MIX), F32)
        y_ref[...] = jnp.zeros(y_ref.shape, y_ref.dtype)

    _cast_weights(cast_in, cast_out)

    post = _post_pieces(xp_ref, y_ref, p_ref, wout_ref, g1_ref, b1_ref, wup_ref, wdn_ref,
                        g2_ref, b2_ref, wg_ref, bg_ref, wp_ref, o_ref,
                        x1_ref, x1b_ref, act_ref, acc_ref)
    mixer = _odd_mixer_pieces(x_ref, win_ref, bias_ref, poolw_ref, pscale_ref, k_ref, v_ref,
                              zbuf_ref, q_ref, e_ref, r_ref, y_ref)
    _run_pieces(post, mixer)


def _odd_layer(layer, x, p, mixer_args, post_args, casts):
    mixer_specs = [
        _const_spec((D_MODEL, 4 * MIX)),
        _const_spec((ATT_HEADS, ATT_QB, ATT_KB)),
        _const_spec((len(POOL_WINDOWS), POOL_GROUP, POOL_GROUP)),
        _const_spec((1, MIX)),
    ]
    scratch = [
        pltpu.VMEM((ATT_HIST + TM, MIX), BF16), pltpu.VMEM((ATT_HIST + TM, MIX), BF16),
        pltpu.VMEM((POOL_HIST + TM, MIX), F32),
        pltpu.VMEM((TM, MIX), BF16),
        pltpu.VMEM((2, 2, ATT_QB, ATT_KB), BF16),
        pltpu.VMEM((2, 2, ATT_QB, LANES), F32),
    ]
    assert len(mixer_specs) == ODD_MIXER_IN
    return _layer_call(_odd_layer_kernel, "odd_layer", layer, x, p, mixer_args, mixer_specs,
                       post_args, scratch, casts)


def _block_diag(w):
    per = S5_GROUPS // S5_BLOCKS
    a, b = w.shape[1], w.shape[2]
    w = w.reshape(S5_BLOCKS, per, a, b)
    eye = jnp.eye(per, dtype=w.dtype)
    out = w[:, :, :, None, :] * eye[None, :, None, :, None]
    return out.reshape(S5_BLOCKS, per * a, per * b)


def _band_bias(rel_bias):
    assert CHUNK - 1 <= MAX_REL <= LEFT_CHUNKS * CHUNK
    n_far = CHUNK - 1 + LEFT_CHUNKS * CHUNK - MAX_REL
    rev = rel_bias.astype(F32)[:, ::-1]
    by_offset = jnp.concatenate(
        [jnp.broadcast_to(rev[:, :1], (ATT_HEADS, n_far)), rev[:, :BAND + CHUNK - 1 - n_far]], axis=-1)
    compact = jnp.stack([by_offset[:, CHUNK - 1 - qi:CHUNK - 1 - qi + BAND] for qi in range(CHUNK)],
                        axis=1)
    rows = []
    for cq in range(ATT_QB // CHUNK):
        left = jnp.full((ATT_HEADS, CHUNK, cq * CHUNK), NEG_INF, F32)
        right = jnp.full((ATT_HEADS, CHUNK, ATT_KB - BAND - cq * CHUNK), NEG_INF, F32)
        rows.append(jnp.concatenate([left, compact, right], axis=-1))
    return jnp.concatenate(rows, axis=1)


def kernel(x, p, ev_w_in, ev_lambda_re, ev_lambda_im, ev_log_dt, ev_b_re, ev_b_im, ev_c_re, ev_c_im, ev_d, ev_w_glu, ev_b_glu, ev_conv_w, ev_w_out, od_w_in, od_rel_bias, od_pool_w, od_pool_scale, od_w_out, ln_mix_g, ln_mix_b, ln_ffn_g, ln_ffn_b, ffn_w_up, ffn_w_down, ple_w_proj, ple_w_gate, ple_b_gate):
    assert x.shape == (1, SEQ, D_MODEL) and p.shape == (DEPTH, 1, SEQ, D_PLE)
    xs = x.reshape(SEQ, D_MODEL)
    ps = p.reshape(DEPTH, SEQ, D_PLE)
    row = lambda v: v.reshape(1, -1)

    def big_weights(i):
        w_in, w_out = (ev_w_in, ev_w_out) if i % 2 == 0 else (od_w_in, od_w_out)
        return [(w_in, i // 2), (w_out, i // 2), (ffn_w_up, i), (ffn_w_down, i), (ple_w_gate, i)]

    big = [stacked[index].astype(BF16) for stacked, index in big_weights(0)]
    for i in range(DEPTH):
        w_in, w_out, w_up, w_down, w_gate = big
        casts = big_weights(i + 1) if i + 1 < DEPTH else []
        post_args = (w_out, row(ln_mix_g[i]), row(ln_mix_b[i]), w_up, w_down,
                     row(ln_ffn_g[i]), row(ln_ffn_b[i]),
                     w_gate, row(ple_b_gate[i]), ple_w_proj[i].astype(BF16))
        if i % 2 == 0:
            e = i // 2
            mixer_args = (
                w_in,
                row(ev_lambda_re[e]), row(ev_lambda_im[e]),
                row(jnp.repeat(ev_log_dt[e], S5_STATE)),
                _block_diag(jnp.swapaxes(ev_b_re[e], 1, 2)),
                _block_diag(jnp.swapaxes(ev_b_im[e], 1, 2)),
                _block_diag(jnp.swapaxes(ev_c_re[e], 1, 2)).astype(BF16),
                _block_diag(jnp.swapaxes(ev_c_im[e], 1, 2)).astype(BF16),
                row(ev_d[e]), ev_w_glu[e].astype(BF16), row(ev_b_glu[e]), ev_conv_w[e])
            xs, big = _even_layer(i, xs, ps, mixer_args, post_args, casts)
        else:
            o = i // 2
            mixer_args = (w_in, _band_bias(od_rel_bias[o]),
                          od_pool_w[o].astype(BF16), row(od_pool_scale[o]))
            xs, big = _odd_layer(i, xs, ps, mixer_args, post_args, casts)
    return xs.reshape(1, SEQ, D_MODEL)
```

```python
import functools
import math

import jax
import jax.numpy as jnp
from jax import lax
from jax.experimental import pallas as pl
from jax.experimental.pallas import tpu as pltpu

F32 = jnp.float32
BF16 = jnp.bfloat16

D_MODEL = 1024
SEQ = 16384
DEPTH = 4
CHUNK = 64
MIX = D_MODEL // 2
S5_GROUP = 16
S5_GROUPS = MIX // S5_GROUP
S5_STATE = 64
S5_LANES = S5_GROUPS * S5_STATE
CONV_W = 3
ATT_HEADS = 8
HEAD_DIM = MIX // ATT_HEADS
LEFT_CHUNKS = 8
BAND = (LEFT_CHUNKS + 1) * CHUNK
MAX_REL = 128
POOL_WINDOWS = (2, 4, 8, 16)
POOL_GROUP = MIX // len(POOL_WINDOWS)
D_FF = ((math.ceil(8 * D_MODEL / 3) + 255) // 256) * 256
D_PLE = 256
ALPHA = (2 * DEPTH) ** 0.25
LN_EPS = 1e-5
NEG_INF = -1e30

SUBLANES = 8
LANES = 128
BF16_ROWS = 16
VMEM_LIMIT_BYTES = 58 * 1024 * 1024

TM = 256
N_TILES = SEQ // TM
FF_CHUNK = 256
N_FF = D_FF // FF_CHUNK
S5_BLOCK_IN = LANES
S5_BLOCKS = MIX // S5_BLOCK_IN
S5_BLOCK_STATE = S5_LANES // S5_BLOCKS
SCAN_BLOCKS = TM // SUBLANES
SCAN_SLAB = 512
SCAN_PIECE = 4
SCAN_DELAY = 2
ATT_DELAY = 4
ATT_QB = 2 * CHUNK
ATT_HIST = LEFT_CHUNKS * CHUNK
ATT_KB = ATT_HIST + ATT_QB
EVEN_MIXER_IN = 12
ODD_MIXER_IN = 4
POOL_HIST = 16
CONV_HIST = SUBLANES


def _dot(a, b):
    return jnp.dot(a, b, preferred_element_type=F32)


def _layer_norm(v, g, b):
    mu = jnp.mean(v, axis=-1, keepdims=True)
    d = v - mu
    var = jnp.mean(d * d, axis=-1, keepdims=True)
    return d * lax.rsqrt(var + LN_EPS) * g + b


def _const_spec(shape):
    zeros = (0,) * len(shape)
    return pl.BlockSpec(shape, lambda i: zeros, pipeline_mode=pl.Buffered(1))


def _run_pieces(post_pieces, mixer_pieces):
    assert len(post_pieces) == len(mixer_pieces)
    for post_piece, mixer_piece in zip(post_pieces, mixer_pieces):
        post_piece()
        mixer_piece()


def _mixer_tile(i):
    return jnp.minimum(i, N_TILES - 1)


def _post_tile(i):
    return jnp.maximum(i - 1, 0)


POST_SPECS = [
    _const_spec((D_MODEL, D_MODEL)), _const_spec((1, D_MODEL)), _const_spec((1, D_MODEL)),
    _const_spec((D_MODEL, 2 * D_FF)), _const_spec((D_FF, D_MODEL)),
    _const_spec((1, D_MODEL)), _const_spec((1, D_MODEL)),
    _const_spec((D_MODEL, D_MODEL)), _const_spec((1, D_MODEL)),
    _const_spec((D_PLE, D_MODEL)),
]
POST_SCRATCH = [
    pltpu.VMEM((TM, D_MODEL), F32),
    pltpu.VMEM((TM, D_MODEL), BF16),
    pltpu.VMEM((2, TM, FF_CHUNK), BF16),
    pltpu.VMEM((TM, D_MODEL), F32),
]


def _post_pieces(x_ref, y_ref, p_ref, wout_ref, g1_ref, b1_ref, wup_ref, wdn_ref,
                 g2_ref, b2_ref, wg_ref, bg_ref, wp_ref, o_ref,
                 x1_ref, x1b_ref, act_ref, acc_ref):
    def head():
        mix = _dot(y_ref[...], wout_ref[...])
        x1 = _layer_norm(ALPHA * x_ref[...] + mix, g1_ref[...], b1_ref[...])
        x1_ref[...] = x1
        x1b_ref[...] = x1.astype(BF16)

    def down(c):
        part = _dot(act_ref[c % 2], wdn_ref[c * FF_CHUNK:(c + 1) * FF_CHUNK, :])
        acc_ref[...] = part if c == 0 else acc_ref[...] + part

    def swiglu(c):
        lo = c * FF_CHUNK
        x1b = x1b_ref[...]
        gate = _dot(x1b, wup_ref[:, lo:lo + FF_CHUNK])
        up = _dot(x1b, wup_ref[:, D_FF + lo:D_FF + lo + FF_CHUNK])
        if c > 0:
            down(c - 1)
        act_ref[c % 2] = (jax.nn.silu(gate) * up).astype(BF16)

    held = {}

    def tail_norm():
        down(N_FF - 1)
        held["emb"] = _dot(p_ref[...].astype(BF16), wp_ref[...])
        held["x2"] = _layer_norm(ALPHA * x1_ref[...] + acc_ref[...], g2_ref[...], b2_ref[...])

    def tail_gate():
        x2 = held["x2"]
        gate = jax.nn.sigmoid(_dot(x2.astype(BF16), wg_ref[...]) + bg_ref[...])
        o_ref[...] = x2 + gate * held["emb"]

    return [head] + [lambda c=c: swiglu(c) for c in range(N_FF)] + [tail_norm, tail_gate]


def _cmul(ar, ai, br, bi):
    return ar * br - ai * bi, ar * bi + ai * br


def _s5_setup(lam_re_ref, lam_im_ref, logdt_ref, braw_re_ref, braw_im_ref,
              tab_ref, bsc_re_ref, bsc_im_ref):
    lre = lam_re_ref[...]
    lim = lam_im_ref[...]
    dt = jnp.exp(logdt_ref[...])
    mag = jnp.exp(lre * dt)
    ang = lim * dt
    lb_re = mag * jnp.cos(ang)
    lb_im = mag * jnp.sin(ang)
    den = lre * lre + lim * lim
    nr = lb_re - 1.0
    ni = lb_im
    r_re = (nr * lre + ni * lim) / den
    r_im = (ni * lre - nr * lim) / den
    for j in range(S5_BLOCKS):
        cols = slice(j * S5_BLOCK_STATE, (j + 1) * S5_BLOCK_STATE)
        b_re = braw_re_ref[j]
        b_im = braw_im_ref[j]
        bsc_re_ref[j] = (r_re[:, cols] * b_re - r_im[:, cols] * b_im).astype(BF16)
        bsc_im_ref[j] = (r_re[:, cols] * b_im + r_im[:, cols] * b_re).astype(BF16)
    powers = [(lb_re, lb_im)]
    for _ in range(SUBLANES - 1):
        powers.append(_cmul(powers[-1][0], powers[-1][1], lb_re, lb_im))
    row = lax.broadcasted_iota(jnp.int32, (SUBLANES, S5_LANES), 0)
    zero = jnp.zeros((SUBLANES, S5_LANES), F32)
    for t, d in enumerate((1, 2, 4)):
        pr, pi = powers[d - 1]
        tab_ref[2 * t] = jnp.where(row >= d, jnp.broadcast_to(pr, zero.shape), zero)
        tab_ref[2 * t + 1] = jnp.where(row >= d, jnp.broadcast_to(pi, zero.shape), zero)
    car_re = zero
    car_im = zero
    for k in range(SUBLANES):
        car_re = jnp.where(row == k, jnp.broadcast_to(powers[k][0], zero.shape), car_re)
        car_im = jnp.where(row == k, jnp.broadcast_to(powers[k][1], zero.shape), car_im)
    tab_ref[6] = car_re
    tab_ref[7] = car_im


def _s5_scan_blocks(first, last, tab_ref, bu_re_ref, bu_im_ref, carry_re_ref, carry_im_ref):
    base = jnp.minimum(pl.program_id(0), 0) * SUBLANES
    for s in range(S5_LANES // SCAN_SLAB):
        cols = slice(s * SCAN_SLAB, (s + 1) * SCAN_SLAB)
        mult = [(tab_ref[2 * t, :, cols], tab_ref[2 * t + 1, :, cols]) for t in range(3)]
        pw_re = tab_ref[6, :, cols]
        pw_im = tab_ref[7, :, cols]
        c_re = carry_re_ref[:, cols]
        c_im = carry_im_ref[:, cols]
        for b in range(first, last):
            rows = pl.ds(pl.multiple_of(base + b * SUBLANES, SUBLANES), SUBLANES)
            re = bu_re_ref[rows, cols]
            im = bu_im_ref[rows, cols]
            for (a_re, a_im), d in zip(mult, (1, 2, 4)):
                m_re, m_im = _cmul(a_re, a_im, pltpu.roll(re, d, 0), pltpu.roll(im, d, 0))
                re = re + m_re
                im = im + m_im
            m_re, m_im = _cmul(pw_re, pw_im, c_re, c_im)
            re = re + m_re
            im = im + m_im
            bu_re_ref[rows, cols] = re
            bu_im_ref[rows, cols] = im
            top = slice(SUBLANES - 1, SUBLANES)
            c_re = jnp.broadcast_to(re[top, :], re.shape)
            c_im = jnp.broadcast_to(im[top, :], im.shape)
        carry_re_ref[:, cols] = c_re
        carry_im_ref[:, cols] = c_im


def _even_mixer_pieces(x_ref, win_ref, bsc_re_ref, bsc_im_ref, c_re_ref, c_im_ref, dskip_ref,
                       wglu_ref, bglu_ref, convw_ref, tab_ref, bu_re_ref, bu_im_ref,
                       carry_re_ref, carry_im_ref, zbuf_ref, u_ref, bgate_ref, y_ref):
    def head():
        h = _dot(x_ref[...].astype(BF16), win_ref[...])
        u = h[:, 0:MIX]
        u_ref[...] = u
        ub = u.astype(BF16)
        for j in range(S5_BLOCKS):
            uj = ub[:, j * S5_BLOCK_IN:(j + 1) * S5_BLOCK_IN]
            cols = slice(j * S5_BLOCK_STATE, (j + 1) * S5_BLOCK_STATE)
            bu_re_ref[:, cols] = _dot(uj, bsc_re_ref[j])
            bu_im_ref[:, cols] = _dot(uj, bsc_im_ref[j])
        bgate_ref[...] = h[:, MIX:2 * MIX]
        zbuf_ref[CONV_HIST:CONV_HIST + TM, :] = h[:, 2 * MIX:3 * MIX] * h[:, 3 * MIX:4 * MIX]

    def conv():
        taps = convw_ref[CONV_W - 1:CONV_W, :] * zbuf_ref[CONV_HIST:CONV_HIST + TM, :]
        for k in range(CONV_W - 1):
            lag = CONV_W - 1 - k
            taps = taps + convw_ref[k:k + 1, :] * zbuf_ref[CONV_HIST - lag:CONV_HIST - lag + TM, :]
        y_ref[:, MIX:2 * MIX] = (bgate_ref[...] * taps).astype(y_ref.dtype)
        zbuf_ref[0:CONV_HIST, :] = zbuf_ref[TM:TM + CONV_HIST, :]

    def scan(first):
        _s5_scan_blocks(first, min(first + SCAN_PIECE, SCAN_BLOCKS), tab_ref, bu_re_ref,
                        bu_im_ref, carry_re_ref, carry_im_ref)

    held = {}

    def tail_readout():
        ys = []
        for j in range(S5_BLOCKS):
            cols = slice(j * S5_BLOCK_STATE, (j + 1) * S5_BLOCK_STATE)
            ys.append(_dot(bu_re_ref[:, cols].astype(BF16), c_re_ref[j])
                      - _dot(bu_im_ref[:, cols].astype(BF16), c_im_ref[j]))
        y = jnp.concatenate(ys, axis=-1) + dskip_ref[...] * u_ref[...]
        held["g"] = jax.nn.gelu(y)

    def tail_glu():
        g = held["g"]
        out_a = g * jax.nn.sigmoid(_dot(g.astype(BF16), wglu_ref[...]) + bglu_ref[...])
        y_ref[:, 0:MIX] = out_a.astype(y_ref.dtype)

    scans = [lambda f=f: scan(f) for f in range(0, SCAN_BLOCKS, SCAN_PIECE)]
    idle = lambda: None
    assert SCAN_DELAY > 0
    pieces = [head, conv] + [idle] * (SCAN_DELAY - 1) + scans
    return pieces + [idle] * (N_FF + 1 - len(pieces)) + [tail_readout, tail_glu]


def _split_refs(refs, n_mixer_in, n_cast):
    refs = list(refs)
    sizes = (3, n_mixer_in, len(POST_SPECS), n_cast, 1, n_cast)
    groups = []
    for size in sizes:
        groups.append(refs[:size])
        refs = refs[size:]
    return groups + [refs]


def _cast_weights(cast_in, cast_out):
    for w_ref, o_ref in zip(cast_in, cast_out):
        o_ref[...] = w_ref[...].astype(BF16)


def _even_layer_kernel(n_cast, *refs):
    (x_ref, xp_ref, p_ref), mixer_in, post_in, cast_in, (o_ref,), cast_out, scratch = _split_refs(
        refs, EVEN_MIXER_IN, n_cast)
    (win_ref, lam_re_ref, lam_im_ref, logdt_ref, braw_re_ref, braw_im_ref, c_re_ref, c_im_ref,
     dskip_ref, wglu_ref, bglu_ref, convw_ref) = mixer_in
    wout_ref, g1_ref, b1_ref, wup_ref, wdn_ref, g2_ref, b2_ref, wg_ref, bg_ref, wp_ref = post_in
    (y_ref, x1_ref, x1b_ref, act_ref, acc_ref, tab_ref, bsc_re_ref, bsc_im_ref, bu_re_ref,
     bu_im_ref, carry_re_ref, carry_im_ref, zbuf_ref, u_ref, bgate_ref) = scratch

    @pl.when(pl.program_id(0) == 0)
    def _():
        _s5_setup(lam_re_ref, lam_im_ref, logdt_ref, braw_re_ref, braw_im_ref,
                  tab_ref, bsc_re_ref, bsc_im_ref)
        carry_re_ref[...] = jnp.zeros(carry_re_ref.shape, F32)
        carry_im_ref[...] = jnp.zeros(carry_im_ref.shape, F32)
        zbuf_ref[0:CONV_HIST, :] = jnp.zeros((CONV_HIST, MIX), F32)
        y_ref[...] = jnp.zeros(y_ref.shape, y_ref.dtype)

    _cast_weights(cast_in, cast_out)

    post = _post_pieces(xp_ref, y_ref, p_ref, wout_ref, g1_ref, b1_ref, wup_ref, wdn_ref,
                        g2_ref, b2_ref, wg_ref, bg_ref, wp_ref, o_ref,
                        x1_ref, x1b_ref, act_ref, acc_ref)
    mixer = _even_mixer_pieces(x_ref, win_ref, bsc_re_ref, bsc_im_ref, c_re_ref, c_im_ref,
                               dskip_ref, wglu_ref, bglu_ref, convw_ref, tab_ref, bu_re_ref,
                               bu_im_ref, carry_re_ref, carry_im_ref, zbuf_ref, u_ref,
                               bgate_ref, y_ref)
    _run_pieces(post, mixer)


def _cast_specs(casts):
    in_specs, out_specs, out_shapes = [], [], []
    for stacked, index in casts:
        _, rows, cols = stacked.shape
        block = next(b for b in range(BF16_ROWS, rows + 1, BF16_ROWS)
                     if rows % b == 0 and rows // b <= N_TILES)
        last = rows // block - 1
        in_specs.append(pl.BlockSpec(
            (None, block, cols), lambda i, index=index, last=last: (index, jnp.minimum(i, last), 0)))
        out_specs.append(pl.BlockSpec(
            (block, cols), lambda i, last=last: (jnp.minimum(i, last), 0)))
        out_shapes.append(jax.ShapeDtypeStruct((rows, cols), BF16))
    return in_specs, out_specs, out_shapes


def _layer_call(kernel_fn, name, layer, x, p, mixer_args, mixer_specs, post_args, scratch,
                casts):
    cast_in_specs, cast_out_specs, cast_shapes = _cast_specs(casts)
    out, *cast = pl.pallas_call(
        functools.partial(kernel_fn, len(casts)),
        grid=(N_TILES + 1,),
        in_specs=[
            pl.BlockSpec((TM, D_MODEL), lambda i: (_mixer_tile(i), 0)),
            pl.BlockSpec((TM, D_MODEL), lambda i: (_post_tile(i), 0)),
            pl.BlockSpec((None, TM, D_PLE), lambda i: (layer, _post_tile(i), 0)),
        ] + mixer_specs + POST_SPECS + cast_in_specs,
        out_specs=[pl.BlockSpec((TM, D_MODEL), lambda i: (_post_tile(i), 0))] + cast_out_specs,
        out_shape=[jax.ShapeDtypeStruct((SEQ, D_MODEL), F32)] + cast_shapes,
        scratch_shapes=[pltpu.VMEM((TM, D_MODEL), BF16)] + POST_SCRATCH + scratch,
        compiler_params=pltpu.CompilerParams(dimension_semantics=("arbitrary",),
                                             vmem_limit_bytes=VMEM_LIMIT_BYTES),
        name=name,
    )(x, x, p, *mixer_args, *post_args, *[stacked for stacked, _ in casts])
    return out, cast


def _even_layer(layer, x, p, mixer_args, post_args, casts):
    blk_b = (S5_BLOCKS, S5_BLOCK_IN, S5_BLOCK_STATE)
    blk_c = (S5_BLOCKS, S5_BLOCK_STATE, S5_BLOCK_IN)
    mixer_specs = [
        _const_spec((D_MODEL, 4 * MIX)),
        _const_spec((1, S5_LANES)), _const_spec((1, S5_LANES)), _const_spec((1, S5_LANES)),
        _const_spec(blk_b), _const_spec(blk_b), _const_spec(blk_c), _const_spec(blk_c),
        _const_spec((1, MIX)), _const_spec((MIX, MIX)), _const_spec((1, MIX)),
        _const_spec((CONV_W, MIX)),
    ]
    scratch = [
        pltpu.VMEM((8, SUBLANES, S5_LANES), F32),
        pltpu.VMEM(blk_b, BF16), pltpu.VMEM(blk_b, BF16),
        pltpu.VMEM((TM, S5_LANES), F32), pltpu.VMEM((TM, S5_LANES), F32),
        pltpu.VMEM((SUBLANES, S5_LANES), F32), pltpu.VMEM((SUBLANES, S5_LANES), F32),
        pltpu.VMEM((CONV_HIST + TM, MIX), F32),
        pltpu.VMEM((TM, MIX), F32),
        pltpu.VMEM((TM, MIX), F32),
    ]
    assert len(mixer_specs) == EVEN_MIXER_IN
    return _layer_call(_even_layer_kernel, "even_layer", layer, x, p, mixer_args, mixer_specs,
                       post_args, scratch, casts)


def _odd_mixer_pieces(x_ref, win_ref, bias_ref, poolw_ref, pscale_ref, k_ref, v_ref, zbuf_ref,
                      q_ref, e_ref, r_ref, y_ref):
    i = pl.program_id(0)

    def head():
        h = _dot(x_ref[...].astype(BF16), win_ref[...])
        q_ref[...] = (h[:, 0:MIX] * (HEAD_DIM ** -0.5)).astype(BF16)
        k_ref[ATT_HIST:ATT_HIST + TM, :] = h[:, MIX:2 * MIX].astype(BF16)
        v_ref[ATT_HIST:ATT_HIST + TM, :] = h[:, 2 * MIX:3 * MIX].astype(BF16)
        zbuf_ref[POOL_HIST:POOL_HIST + TM, :] = h[:, 3 * MIX:4 * MIX]

    def pool():
        z = zbuf_ref[POOL_HIST:POOL_HIST + TM, :]
        t_pos = i * TM + lax.broadcasted_iota(jnp.int32, (TM, 1), 0)
        mixed = []
        for gi, w in enumerate(POOL_WINDOWS):
            cols = slice(gi * POOL_GROUP, (gi + 1) * POOL_GROUP)
            zg = z[:, cols]
            total = zg
            for lag in range(1, w):
                total = total + zbuf_ref[POOL_HIST - lag:POOL_HIST - lag + TM, cols]
            count = jnp.minimum(t_pos + 1, w).astype(F32)
            pooled = total / count - zg
            mixed.append(_dot(pooled.astype(BF16), poolw_ref[gi]))
        y_d = jnp.concatenate(mixed, axis=-1) * pscale_ref[...]
        y_ref[:, MIX:2 * MIX] = y_d.astype(y_ref.dtype)
        zbuf_ref[0:POOL_HIST, :] = zbuf_ref[TM:TM + POOL_HIST, :]

    units = [(qb, j) for qb in range(TM // ATT_QB) for j in range(MIX // LANES)]
    lane = lambda: lax.broadcasted_iota(jnp.int32, (ATT_QB, LANES), 1)

    def scores(n):
        qb, j = units[n]
        r0 = qb * ATT_QB
        first_key = i * TM - ATT_HIST + r0
        key_col = lax.broadcasted_iota(jnp.int32, (ATT_QB, ATT_KB), 1)
        started = key_col + first_key >= 0
        cols = slice(j * LANES, (j + 1) * LANES)
        q2 = q_ref[r0:r0 + ATT_QB, cols]
        k2 = k_ref[r0:r0 + ATT_KB, cols]
        for half in range(2):
            in_head = (lane() >= half * HEAD_DIM) & (lane() < (half + 1) * HEAD_DIM)
            qm = jnp.where(in_head, q2, jnp.zeros_like(q2))
            s = lax.dot_general(qm, k2, (((1,), (1,)), ((), ())), preferred_element_type=F32)
            s = jnp.where(started, s + bias_ref[2 * j + half], NEG_INF)
            m = jnp.max(s, axis=-1, keepdims=True)
            e = jnp.exp(s - m)
            denom = jnp.sum(e, axis=-1, keepdims=True)
            e_ref[n % 2, half] = e.astype(BF16)
            r_ref[n % 2, half] = jnp.broadcast_to(1.0 / denom, (ATT_QB, LANES))

    def context(n):
        qb, j = units[n]
        r0 = qb * ATT_QB
        cols = slice(j * LANES, (j + 1) * LANES)
        v2 = v_ref[r0:r0 + ATT_KB, cols]
        halves = [_dot(e_ref[n % 2, half], v2) * r_ref[n % 2, half] for half in range(2)]
        o2 = jnp.where(lane() < HEAD_DIM, halves[0], halves[1])
        y_ref[r0:r0 + ATT_QB, cols] = o2.astype(y_ref.dtype)

    def attend(n):
        if 0 <= n < len(units):
            scores(n)
        if 0 < n <= len(units):
            context(n - 1)

    def shift_history():
        k_ref[0:ATT_HIST, :] = k_ref[TM:TM + ATT_HIST, :]
        v_ref[0:ATT_HIST, :] = v_ref[TM:TM + ATT_HIST, :]

    assert ATT_DELAY == N_FF + 1 - len(units) and ATT_DELAY > 0
    pieces = [lambda k=k: attend(k - ATT_DELAY) for k in range(N_FF + 2)]
    pieces[0] = pool
    last = pieces[-1]
    pieces[-1] = lambda: (last(), shift_history())
    return [head] + pieces


def _odd_layer_kernel(n_cast, *refs):
    (x_ref, xp_ref, p_ref), mixer_in, post_in, cast_in, (o_ref,), cast_out, scratch = _split_refs(
        refs, ODD_MIXER_IN, n_cast)
    win_ref, bias_ref, poolw_ref, pscale_ref = mixer_in
    wout_ref, g1_ref, b1_ref, wup_ref, wdn_ref, g2_ref, b2_ref, wg_ref, bg_ref, wp_ref = post_in
    (y_ref, x1_ref, x1b_ref, act_ref, acc_ref, k_ref, v_ref, zbuf_ref, q_ref, e_ref,
     r_ref) = scratch

    @pl.when(pl.program_id(0) == 0)
    def _():
        k_ref[0:ATT_HIST, :] = jnp.zeros((ATT_HIST, MIX), BF16)
        v_ref[0:ATT_HIST, :] = jnp.zeros((ATT_HIST, MIX), BF16)
        zbuf_ref[0:POOL_HIST, :] = jnp.zeros((POOL_HIST, MIX), F32)
        y_ref[...] = jnp.zeros(y_ref.shape, y_ref.dtype)

    _cast_weights(cast_in, cast_out)

    post = _post_pieces(xp_ref, y_ref, p_ref, wout_ref, g1_ref, b1_ref, wup_ref, wdn_ref,
                        g2_ref, b2_ref, wg_ref, bg_ref, wp_ref, o_ref,
                        x1_ref, x1b_ref, act_ref, acc_ref)
    mixer = _odd_mixer_pieces(x_ref, win_ref, bias_ref, poolw_ref, pscale_ref, k_ref, v_ref,
                              zbuf_ref, q_ref, e_ref, r_ref, y_ref)
    _run_pieces(post, mixer)


def _odd_layer(layer, x, p, mixer_args, post_args, casts):
    mixer_specs = [
        _const_spec((D_MODEL, 4 * MIX)),
        _const_spec((ATT_HEADS, ATT_QB, ATT_KB)),
        _const_spec((len(POOL_WINDOWS), POOL_GROUP, POOL_GROUP)),
        _const_spec((1, MIX)),
    ]
    scratch = [
        pltpu.VMEM((ATT_HIST + TM, MIX), BF16), pltpu.VMEM((ATT_HIST + TM, MIX), BF16),
        pltpu.VMEM((POOL_HIST + TM, MIX), F32),
        pltpu.VMEM((TM, MIX), BF16),
        pltpu.VMEM((2, 2, ATT_QB, ATT_KB), BF16),
        pltpu.VMEM((2, 2, ATT_QB, LANES), F32),
    ]
    assert len(mixer_specs) == ODD_MIXER_IN
    return _layer_call(_odd_layer_kernel, "odd_layer", layer, x, p, mixer_args, mixer_specs,
                       post_args, scratch, casts)


def _block_diag(w):
    per = S5_GROUPS // S5_BLOCKS
    a, b = w.shape[1], w.shape[2]
    w = w.reshape(S5_BLOCKS, per, a, b)
    eye = jnp.eye(per, dtype=w.dtype)
    out = w[:, :, :, None, :] * eye[None, :, None, :, None]
    return out.reshape(S5_BLOCKS, per * a, per * b)


def _band_bias(rel_bias):
    assert CHUNK - 1 <= MAX_REL <= LEFT_CHUNKS * CHUNK
    n_far = CHUNK - 1 + LEFT_CHUNKS * CHUNK - MAX_REL
    rev = rel_bias.astype(F32)[:, ::-1]
    by_offset = jnp.concatenate(
        [jnp.broadcast_to(rev[:, :1], (ATT_HEADS, n_far)), rev[:, :BAND + CHUNK - 1 - n_far]], axis=-1)
    compact = jnp.stack([by_offset[:, CHUNK - 1 - qi:CHUNK - 1 - qi + BAND] for qi in range(CHUNK)],
                        axis=1)
    rows = []
    for cq in range(ATT_QB // CHUNK):
        left = jnp.full((ATT_HEADS, CHUNK, cq * CHUNK), NEG_INF, F32)
        right = jnp.full((ATT_HEADS, CHUNK, ATT_KB - BAND - cq * CHUNK), NEG_INF, F32)
        rows.append(jnp.concatenate([left, compact, right], axis=-1))
    return jnp.concatenate(rows, axis=1)


def kernel(x, p, ev_w_in, ev_lambda_re, ev_lambda_im, ev_log_dt, ev_b_re, ev_b_im, ev_c_re, ev_c_im, ev_d, ev_w_glu, ev_b_glu, ev_conv_w, ev_w_out, od_w_in, od_rel_bias, od_pool_w, od_pool_scale, od_w_out, ln_mix_g, ln_mix_b, ln_ffn_g, ln_ffn_b, ffn_w_up, ffn_w_down, ple_w_proj, ple_w_gate, ple_b_gate):
    assert x.shape == (1, SEQ, D_MODEL) and p.shape == (DEPTH, 1, SEQ, D_PLE)
    xs = x.reshape(SEQ, D_MODEL)
    ps = p.reshape(DEPTH, SEQ, D_PLE)
    row = lambda v: v.reshape(1, -1)

    def big_weights(i):
        w_in, w_out = (ev_w_in, ev_w_out) if i % 2 == 0 else (od_w_in, od_w_out)
        return [(w_in, i // 2), (w_out, i // 2), (ffn_w_up, i), (ffn_w_down, i), (ple_w_gate, i)]

    big = [stacked[index].astype(BF16) for stacked, index in big_weights(0)]
    for i in range(DEPTH):
        w_in, w_out, w_up, w_down, w_gate = big
        casts = big_weights(i + 1) if i + 1 < DEPTH else []
        post_args = (w_out, row(ln_mix_g[i]), row(ln_mix_b[i]), w_up, w_down,
                     row(ln_ffn_g[i]), row(ln_ffn_b[i]),
                     w_gate, row(ple_b_gate[i]), ple_w_proj[i].astype(BF16))
        if i % 2 == 0:
            e = i // 2
            mixer_args = (
                w_in,
                row(ev_lambda_re[e]), row(ev_lambda_im[e]),
                row(jnp.repeat(ev_log_dt[e], S5_STATE)),
                _block_diag(jnp.swapaxes(ev_b_re[e], 1, 2)),
                _block_diag(jnp.swapaxes(ev_b_im[e], 1, 2)),
                _block_diag(jnp.swapaxes(ev_c_re[e], 1, 2)).astype(BF16),
                _block_diag(jnp.swapaxes(ev_c_im[e], 1, 2)).astype(BF16),
                row(ev_d[e]), ev_w_glu[e].astype(BF16), row(ev_b_glu[e]), ev_conv_w[e])
            xs, big = _even_layer(i, xs, ps, mixer_args, post_args, casts)
        else:
            o = i // 2
            mixer_args = (w_in, _band_bias(od_rel_bias[o]),
                          od_pool_w[o].astype(BF16), row(od_pool_scale[o]))
            xs, big = _odd_layer(i, xs, ps, mixer_args, post_args, casts)
    return xs.reshape(1, SEQ, D_MODEL)
```
